```python
import math
import jax
import jax.numpy as jnp
from jax import lax
import numpy as np

D_MODEL = 1024
BATCH = 2
SEQ = 16384
DEPTH = 2

HEAD_DIM = 64
HEADS_PER_GROUP = 8
DILATED_GROUPS = ((128, 1), (512, 4), (2048, 16))
N_ATTN_GROUPS = len(DILATED_GROUPS)
ATTN_WIDTH = N_ATTN_GROUPS * HEADS_PER_GROUP * HEAD_DIM
ATTN_OUT_WIDTH = HEADS_PER_GROUP * HEAD_DIM
QUERY_BLOCK = 128
ROPE_THETA = 10000.0
LRU_WIDTH = D_MODEL
LRU_BLOCKS = 16
LRU_BLOCK_WIDTH = LRU_WIDTH // LRU_BLOCKS
CONV_WIDTH = 4
LRU_C = 8.0
N_BRANCHES = 2
N_IN = 3 * ATTN_WIDTH + 2 * LRU_WIDTH + N_BRANCHES * D_MODEL
N_GROUPS = 4
EXPERTS_PER_GROUP = 4
N_EXPERTS = N_GROUPS * EXPERTS_PER_GROUP
TOP_K_IN_GROUP = 2
D_EXPERT = 512
EPS = 1e-6

kernel_name = "hybrid_dilated_attn_rglru_hmoe"


def rmsnorm(x, scale):
    x32 = x.astype(jnp.float32)
    y = x32 * lax.rsqrt(jnp.mean(x32 * x32, axis=-1, keepdims=True) + EPS)
    return (y * scale.astype(jnp.float32)).astype(x.dtype)


def rotary_tables(seq_len):
    inv = ROPE_THETA ** (-jnp.arange(0, HEAD_DIM, 2, dtype=jnp.float32) / HEAD_DIM)
    ang = jnp.arange(seq_len, dtype=jnp.float32)[:, None] * inv[None, :]
    return jnp.cos(ang), jnp.sin(ang)


def apply_rope(x, cos, sin):
    half = x.shape[-1] // 2
    x32 = x.astype(jnp.float32)
    x1, x2 = x32[..., :half], x32[..., half:]
    c, s = cos[None, :, None, :], sin[None, :, None, :]
    return jnp.concatenate([x1 * c - x2 * s, x2 * c + x1 * s], axis=-1).astype(x.dtype)


def dilated_window_attention(q, k, v, window, dilation):
    B, S, H, E = q.shape
    L = S // dilation
    W = window // dilation
    Q = math.gcd(L, QUERY_BLOCK)
    nblk = L // Q
    qb = q.reshape(B, nblk, Q, dilation, H, E)
    pad = ((0, 0), (W, 0), (0, 0), (0, 0), (0, 0))
    kp = jnp.pad(k.reshape(B, L, dilation, H, E), pad)
    vp = jnp.pad(v.reshape(B, L, dilation, H, E), pad)
    key_idx = jnp.arange(nblk)[:, None] * Q + jnp.arange(Q + W)[None, :]
    kb = kp[:, key_idx]
    vb = vp[:, key_idx]
    s = jnp.einsum('bnqrhe,bnkrhe->bnrhqk', qb, kb).astype(jnp.float32) * (HEAD_DIM ** -0.5)
    dist = jnp.arange(Q)[:, None] + W - jnp.arange(Q + W)[None, :]
    valid = (dist >= 0) & (dist <= W) & (key_idx[:, None, :] >= W)
    s = jnp.where(valid[None, :, None, None], s, -jnp.inf)
    mx = jnp.max(s, axis=-1, keepdims=True)
    p = jnp.exp(s - mx)
    den = jnp.sum(p, axis=-1, keepdims=True)
    o = jnp.einsum('bnrhqk,bnkrhe->bnqrhe', (p / den).astype(v.dtype), vb)
    lse = (mx + jnp.log(den))[..., 0]
    o = o.reshape(B, S, H, E)
    lse = jnp.transpose(lse, (0, 1, 4, 2, 3)).reshape(B, S, H)
    return o, lse


def rg_lru(x, wa, ba, wx, bx, lam):
    B, S, _ = x.shape
    xb = x.reshape(B, S, LRU_BLOCKS, LRU_BLOCK_WIDTH)
    r = jax.nn.sigmoid(jnp.einsum('bsni,nij->bsnj', xb, wa).reshape(B, S, LRU_WIDTH) + ba)
    i = jax.nn.sigmoid(jnp.einsum('bsni,nij->bsnj', xb, wx).reshape(B, S, LRU_WIDTH) + bx)
    log_a = -LRU_C * r.astype(jnp.float32) * jax.nn.softplus(-lam.astype(jnp.float32))
    a = jnp.exp(log_a)
    b = jnp.sqrt(-jnp.expm1(2.0 * log_a)) * (i * x).astype(jnp.float32)

    def combine(left, right):
        al, bl = left
        ar, br = right
        return al * ar, ar * bl + br

    _, h = lax.associative_scan(combine, (a, b), axis=1)
    return h.astype(x.dtype)


def moe_ffn(h, wg, bg, we, be, w_gate, w_up, w_down):
    B, S, D = h.shape
    xt = h.reshape(B * S, D)
    g_logits = (xt @ wg + bg).astype(jnp.float32)
    g_prob = jax.nn.softmax(g_logits, axis=-1)
    g_idx = jnp.argmax(g_logits, axis=-1)
    g_w = jnp.take_along_axis(g_prob, g_idx[:, None], axis=-1)
    e_logits = (xt @ we + be).astype(jnp.float32).reshape(-1, N_GROUPS, EXPERTS_PER_GROUP)
    e_sel = jnp.take_along_axis(e_logits, g_idx[:, None, None], axis=1)[:, 0]
    top_v, top_i = lax.top_k(e_sel, TOP_K_IN_GROUP)
    w_top = jax.nn.softmax(top_v, axis=-1) * g_w
    expert_id = g_idx[:, None] * EXPERTS_PER_GROUP + top_i
    combine = jnp.sum(jax.nn.one_hot(expert_id, N_EXPERTS, dtype=jnp.float32) * w_top[..., None], axis=1)
    out = jnp.zeros((B * S, D), jnp.float32)
    for e in range(N_EXPERTS):
        act = jax.nn.silu(xt @ w_gate[e]) * (xt @ w_up[e])
        out = out + combine[:, e:e + 1] * (act @ w_down[e]).astype(jnp.float32)
    return out.reshape(B, S, D).astype(h.dtype)


def hybrid_layer(x, cos, sin, norm1, w_in, b_merge, conv_w, conv_b, lru_wa, lru_ba, lru_wx, lru_bx,
                 lru_lambda, w_proj_attn, w_proj_lru, w_out, norm2, router_group_w, router_group_b,
                 router_expert_w, router_expert_b, expert_w_gate, expert_w_up, expert_w_down):
    B, S, _ = x.shape
    h = rmsnorm(x, norm1)
    proj = h @ w_in
    cuts = [ATTN_WIDTH, 2 * ATTN_WIDTH, 3 * ATTN_WIDTH, 3 * ATTN_WIDTH + LRU_WIDTH, 3 * ATTN_WIDTH + 2 * LRU_WIDTH]
    q, k, v, u, g_lru, merge_logits = jnp.split(proj, cuts, axis=-1)
    n_heads = N_ATTN_GROUPS * HEADS_PER_GROUP
    q = apply_rope(q.reshape(B, S, n_heads, HEAD_DIM), cos, sin).reshape(B, S, N_ATTN_GROUPS, HEADS_PER_GROUP, HEAD_DIM)
    k = apply_rope(k.reshape(B, S, n_heads, HEAD_DIM), cos, sin).reshape(B, S, N_ATTN_GROUPS, HEADS_PER_GROUP, HEAD_DIM)
    v = v.reshape(B, S, N_ATTN_GROUPS, HEADS_PER_GROUP, HEAD_DIM)
    outs, lses = [], []
    for gi, (window, dilation) in enumerate(DILATED_GROUPS):
        o, lse = dilated_window_attention(q[:, :, gi], k[:, :, gi], v[:, :, gi], window, dilation)
        outs.append(o)
        lses.append(lse)
    alpha = jax.nn.softmax(jnp.stack(lses, axis=0), axis=0)
    attn = jnp.einsum('gbsh,gbshe->bshe', alpha, jnp.stack(outs, axis=0).astype(jnp.float32))
    attn = attn.reshape(B, S, ATTN_OUT_WIDTH).astype(x.dtype)
    u_pad = jnp.pad(u, ((0, 0), (CONV_WIDTH - 1, 0), (0, 0)))
    u_conv = sum(u_pad[:, j:j + S] * conv_w[j] for j in range(CONV_WIDTH)) + conv_b
    rec = rg_lru(u_conv, lru_wa, lru_ba, lru_wx, lru_bx, lru_lambda)
    lru_out = jax.nn.gelu(g_lru) * rec
    gates = jax.nn.sigmoid(merge_logits + b_merge).reshape(B, S, N_BRANCHES, D_MODEL)
    mixed = gates[:, :, 0] * (attn @ w_proj_attn) + gates[:, :, 1] * (lru_out @ w_proj_lru)
    x = x + mixed @ w_out
    h2 = rmsnorm(x, norm2)
    x = x + moe_ffn(h2, router_group_w, router_group_b, router_expert_w, router_expert_b,
                    expert_w_gate, expert_w_up, expert_w_down)
    return x


def setup_inputs(seed: int = 0) -> dict:
    key = jax.random.key(seed)
    ks = jax.random.split(key, 24)
    f32 = jnp.float32
    nrm = lambda k, shape, fan_in: jax.random.normal(k, shape, f32) * (fan_in ** -0.5)
    u = jax.random.uniform(ks[9], (DEPTH, LRU_WIDTH), f32, minval=0.9, maxval=0.999)
    a0 = u ** (1.0 / LRU_C)
    return {
        "x": jax.random.normal(ks[0], (BATCH, SEQ, D_MODEL), f32),
        "norm1_scale": 1.0 + 0.02 * jax.random.normal(ks[1], (DEPTH, D_MODEL), f32),
        "w_in": nrm(ks[2], (DEPTH, D_MODEL, N_IN), D_MODEL),
        "b_merge": 0.02 * jax.random.normal(ks[3], (DEPTH, N_BRANCHES * D_MODEL), f32),
        "conv_w": nrm(ks[4], (DEPTH, CONV_WIDTH, LRU_WIDTH), CONV_WIDTH),
        "conv_b": 0.02 * jax.random.normal(ks[5], (DEPTH, LRU_WIDTH), f32),
        "lru_wa": nrm(ks[6], (DEPTH, LRU_BLOCKS, LRU_BLOCK_WIDTH, LRU_BLOCK_WIDTH), LRU_BLOCK_WIDTH),
        "lru_ba": 0.02 * jax.random.normal(ks[7], (DEPTH, LRU_WIDTH), f32),
        "lru_wx": nrm(ks[8], (DEPTH, LRU_BLOCKS, LRU_BLOCK_WIDTH, LRU_BLOCK_WIDTH), LRU_BLOCK_WIDTH),
        "lru_bx": 0.02 * jax.random.normal(ks[10], (DEPTH, LRU_WIDTH), f32),
        "lru_lambda": jnp.log(a0) - jnp.log1p(-a0),
        "w_proj_attn": nrm(ks[11], (DEPTH, ATTN_OUT_WIDTH, D_MODEL), ATTN_OUT_WIDTH),
        "w_proj_lru": nrm(ks[12], (DEPTH, LRU_WIDTH, D_MODEL), LRU_WIDTH),
        "w_out": nrm(ks[13], (DEPTH, D_MODEL, D_MODEL), D_MODEL),
        "norm2_scale": 1.0 + 0.02 * jax.random.normal(ks[14], (DEPTH, D_MODEL), f32),
        "router_group_w": nrm(ks[15], (DEPTH, D_MODEL, N_GROUPS), D_MODEL),
        "router_group_b": 0.01 * jax.random.normal(ks[16], (DEPTH, N_GROUPS), f32),
        "router_expert_w": nrm(ks[17], (DEPTH, D_MODEL, N_EXPERTS), D_MODEL),
        "router_expert_b": 0.01 * jax.random.normal(ks[18], (DEPTH, N_EXPERTS), f32),
        "expert_w_gate": nrm(ks[19], (DEPTH, N_EXPERTS, D_MODEL, D_EXPERT), D_MODEL),
        "expert_w_up": nrm(ks[20], (DEPTH, N_EXPERTS, D_MODEL, D_EXPERT), D_MODEL),
        "expert_w_down": nrm(ks[21], (DEPTH, N_EXPERTS, D_EXPERT, D_MODEL), D_EXPERT),
        "final_norm_scale": 1.0 + 0.02 * jax.random.normal(ks[22], (D_MODEL,), f32),
    }


def reference(x, norm1_scale, w_in, b_merge, conv_w, conv_b, lru_wa, lru_ba, lru_wx, lru_bx,
              lru_lambda, w_proj_attn, w_proj_lru, w_out, norm2_scale, router_group_w, router_group_b,
              router_expert_w, router_expert_b, expert_w_gate, expert_w_up, expert_w_down,
              final_norm_scale):
    cos, sin = rotary_tables(x.shape[1])
    for l in range(DEPTH):
        x = hybrid_layer(x, cos, sin, norm1_scale[l], w_in[l], b_merge[l], conv_w[l], conv_b[l],
                         lru_wa[l], lru_ba[l], lru_wx[l], lru_bx[l], lru_lambda[l], w_proj_attn[l],
                         w_proj_lru[l], w_out[l], norm2_scale[l], router_group_w[l], router_group_b[l],
                         router_expert_w[l], router_expert_b[l], expert_w_gate[l], expert_w_up[l],
                         expert_w_down[l])
    return rmsnorm(x, final_norm_scale)
```

```python
import functools

import jax
import jax.numpy as jnp
from jax import lax
from jax.experimental import pallas as pl
from jax.experimental.pallas import tpu as pltpu

D_MODEL = 1024
HEAD_DIM = 64
GROUP_WIDTH = 512
DILATIONS = (1, 4, 16)
WINDOW_STEPS = 128
ATTN_WIDTH = 3 * GROUP_WIDTH
LRU_WIDTH = 1024
LRU_BLOCK = 64
N_IN = 3 * ATTN_WIDTH + 2 * LRU_WIDTH + 2 * D_MODEL
CONV_WIDTH = 4
LRU_C = 8.0
N_GROUPS = 4
EXPERTS_PER_GROUP = 4
N_EXPERTS = 16
D_EXPERT = 512
EPS = 1e-6
ROPE_THETA = 10000.0
LANES = 128
VMEM_LIMIT = 56 * 1024 * 1024

F32 = jnp.float32
BF16 = jnp.bfloat16


def _params(semantics):
    return pltpu.CompilerParams(dimension_semantics=semantics, vmem_limit_bytes=VMEM_LIMIT)


def _in_proj_kernel(x_ref, scale_ref, w_ref, cos_ref, sin_ref, qkv_ref, ug_ref, gl_ref, hn_ref):
    x = x_ref[...]
    ms = jnp.mean(x * x, axis=-1, keepdims=True)
    hn_ref[...] = (x * lax.rsqrt(ms + EPS) * scale_ref[...]).astype(BF16)
    tm = x.shape[0]
    cos = cos_ref[...]
    sin = sin_ref[...]
    lane = lax.broadcasted_iota(jnp.int32, (tm, LANES), 1)
    first_half = (lane % HEAD_DIM) < (HEAD_DIM // 2)
    n_chunks = N_IN // GROUP_WIDTH
    for c in range(n_chunks):
        res = jnp.dot(hn_ref[...], w_ref[:, c * GROUP_WIDTH:(c + 1) * GROUP_WIDTH],
                      preferred_element_type=F32)
        if c < 6:
            parts = []
            for s in range(GROUP_WIDTH // LANES):
                blk = res[:, s * LANES:(s + 1) * LANES]
                partner = jnp.where(first_half, pltpu.roll(blk, LANES - HEAD_DIM // 2, 1),
                                    pltpu.roll(blk, HEAD_DIM // 2, 1))
                o = blk * cos + partner * sin
                if c < 3:
                    o = o * (HEAD_DIM ** -0.5)
                parts.append(o)
            res = jnp.concatenate(parts, axis=1)
        res = res.astype(BF16)
        if c < 9:
            qkv_ref[:, c * GROUP_WIDTH:(c + 1) * GROUP_WIDTH] = res
        elif c < 13:
            ug_ref[:, (c - 9) * GROUP_WIDTH:(c - 8) * GROUP_WIDTH] = res
        else:
            gl_ref[:, (c - 13) * GROUP_WIDTH:(c - 12) * GROUP_WIDTH] = res


def _in_proj(x2, scale, w_bf, cos_t, sin_t, seq):
    t = x2.shape[0]
    tm = 512
    n_seq_tiles = seq // tm
    return pl.pallas_call(
        _in_proj_kernel,
        grid=(t // tm,),
        in_specs=[
            pl.BlockSpec((tm, D_MODEL), lambda i: (i, 0)),
            pl.BlockSpec((1, D_MODEL), lambda i: (0, 0)),
            pl.BlockSpec((D_MODEL, N_IN), lambda i: (0, 0), pipeline_mode=pl.Buffered(1)),
            pl.BlockSpec((tm, LANES), lambda i: (i % n_seq_tiles, 0)),
            pl.BlockSpec((tm, LANES), lambda i: (i % n_seq_tiles, 0)),
        ],
        out_specs=[
            pl.BlockSpec((tm, 3 * ATTN_WIDTH), lambda i: (i, 0)),
            pl.BlockSpec((tm, 2 * LRU_WIDTH), lambda i: (i, 0)),
            pl.BlockSpec((tm, 2 * D_MODEL), lambda i: (i, 0)),
        ],
        out_shape=[
            jax.ShapeDtypeStruct((t, 3 * ATTN_WIDTH), BF16),
            jax.ShapeDtypeStruct((t, 2 * LRU_WIDTH), BF16),
            jax.ShapeDtypeStruct((t, 2 * D_MODEL), BF16),
        ],
        scratch_shapes=[pltpu.VMEM((tm, D_MODEL), BF16)],
        compiler_params=_params(("arbitrary",)),
        name="in_proj",
    )(x2, scale, w_bf, cos_t, sin_t)


def _attn_kernel(q_ref, kp_ref, kc_ref, vp_ref, vc_ref, o_ref, lse_ref, kcat, vcat, *, tq):
    w = WINDOW_STEPS
    kcat[0:w, :] = kp_ref[...]
    kcat[w:, :] = kc_ref[...]
    vcat[0:w, :] = vp_ref[...]
    vcat[w:, :] = vc_ref[...]
    first_tile = pl.program_id(2) == 0

    qi = lax.broadcasted_iota(jnp.int32, (2 * w, 2 * w), 0) % w
    col = lax.broadcasted_iota(jnp.int32, (2 * w, 2 * w), 1)
    band = (col >= qi) & (col <= qi + w)
    lane_q = lax.broadcasted_iota(jnp.int32, (w, LANES), 1)
    low_q = lane_q < HEAD_DIM
    lane_l = lax.broadcasted_iota(jnp.int32, (w, LANES), 1)

    def sub_tile(s, carry):
        r0 = pl.multiple_of(s * w, w)
        min_col = jnp.where(first_tile & (s == 0), w, 0)
        valid = band & (col >= min_col)
        lse_tile = jnp.zeros((w, LANES), F32)
        for p in range(GROUP_WIDTH // LANES):
            lanes = slice(p * LANES, (p + 1) * LANES)
            qp = q_ref[pl.ds(r0, w), lanes]
            zero = jnp.zeros_like(qp)
            qs = jnp.concatenate([jnp.where(low_q, qp, zero), jnp.where(low_q, zero, qp)], axis=0)
            kk = kcat[pl.ds(r0, 2 * w), lanes]
            vv = vcat[pl.ds(r0, 2 * w), lanes]
            sc = lax.dot_general(qs, kk, (((1,), (1,)), ((), ())), preferred_element_type=F32)
            sc = jnp.where(valid, sc, -jnp.inf)
            m = jnp.max(sc, axis=1, keepdims=True)
            pr = jnp.exp(sc - m)
            den = jnp.sum(pr, axis=1, keepdims=True)
            pv = jnp.dot(pr.astype(BF16), vv, preferred_element_type=F32) / den
            o_ref[pl.ds(r0, w), lanes] = jnp.where(low_q, pv[:w], pv[w:]).astype(BF16)
            lse = m + jnp.log(den)
            lse_tile = jnp.where(lane_l == 2 * p, lse[:w], lse_tile)
            lse_tile = jnp.where(lane_l == 2 * p + 1, lse[w:], lse_tile)
        lse_ref[pl.ds(r0, w), :] = lse_tile
        return carry

    lax.fori_loop(0, tq // w, sub_tile, 0)


def _attention_group(qkv, batch, seq, group):
    d = DILATIONS[group]
    length = seq // d
    tq = 512
    w = WINDOW_STEPS
    ncol = 3 * ATTN_WIDTH // GROUP_WIDTH
    view = qkv.reshape(batch, length, d * 3 * ATTN_WIDTH)
    sub = tq // w

    def cur(off):
        return pl.BlockSpec((None, tq, GROUP_WIDTH), lambda b, r, i: (b, i, r * ncol + off + group))

    def prev(off):
        return pl.BlockSpec((None, w, GROUP_WIDTH),
                            lambda b, r, i: (b, jnp.maximum(i * sub - 1, 0), r * ncol + off + group))

    o, lse = pl.pallas_call(
        functools.partial(_attn_kernel, tq=tq),
        grid=(batch, d, length // tq),
        in_specs=[cur(0), prev(3), cur(3), prev(6), cur(6)],
        out_specs=[
            pl.BlockSpec((None, tq, GROUP_WIDTH), lambda b, r, i: (b, i, r)),
            pl.BlockSpec((None, tq, LANES), lambda b, r, i: (b, i, r)),
        ],
        out_shape=[
            jax.ShapeDtypeStruct((batch, length, d * GROUP_WIDTH), BF16),
            jax.ShapeDtypeStruct((batch, length, d * LANES), F32),
        ],
        scratch_shapes=[pltpu.VMEM((tq + w, GROUP_WIDTH), BF16), pltpu.VMEM((tq + w, GROUP_WIDTH), BF16)],
        compiler_params=_params(("arbitrary", "arbitrary", "arbitrary")),
        name=f"attn_d{d}",
    )(view, view, view, view, view)
    return o.reshape(batch * seq, GROUP_WIDTH), lse.reshape(batch * seq, LANES)


def _gelu_tanh(x):
    return 0.5 * x * (1.0 + jnp.tanh(0.7978845608028654 * (x + 0.044715 * (x * x * x))))


def _lru_kernel(u_ref, g_ref, cw_ref, cb_ref, wa_ref, wx_ref, ba_ref, bx_ref, lam_ref, out_ref,
                ext, a_s, b_s, carry, *, ts):
    cw_tile = LRU_WIDTH // 2
    sub = 8

    @pl.when(pl.program_id(2) == 0)
    def _():
        ext[0:sub, :] = jnp.zeros((sub, cw_tile), F32)
        carry[...] = jnp.zeros((sub, cw_tile), F32)

    u = u_ref[...].astype(F32)
    ext[sub:, :] = u
    conv = cb_ref[...] + cw_ref[0:1, :] * ext[sub - 3:sub - 3 + ts, :]
    for j in range(1, CONV_WIDTH):
        conv = conv + cw_ref[j:j + 1, :] * ext[sub - 3 + j:sub - 3 + j + ts, :]
    ext[0:sub, :] = u[ts - sub:, :]

    xb = conv.astype(BF16)
    blk = 4 * LRU_BLOCK
    ra = jnp.concatenate(
        [jnp.dot(xb[:, q * blk:(q + 1) * blk], wa_ref[q], preferred_element_type=F32)
         for q in range(cw_tile // blk)], axis=1)
    rx = jnp.concatenate(
        [jnp.dot(xb[:, q * blk:(q + 1) * blk], wx_ref[q], preferred_element_type=F32)
         for q in range(cw_tile // blk)], axis=1)
    r = jax.nn.sigmoid(ra + ba_ref[...])
    gate_i = jax.nn.sigmoid(rx + bx_ref[...])
    z = -lam_ref[...]
    softplus = jnp.maximum(z, 0.0) + jnp.log1p(jnp.exp(-jnp.abs(z)))
    log_a = (-LRU_C) * r * softplus
    a = jnp.exp(log_a)
    a_s[...] = a
    b_s[...] = jnp.sqrt(1.0 - a * a) * (gate_i * conv)

    row = lax.broadcasted_iota(jnp.int32, (sub, cw_tile), 0)

    def step(k, h_prev):
        r0 = pl.multiple_of(k * sub, sub)
        a8 = a_s[pl.ds(r0, sub), :]
        b8 = b_s[pl.ds(r0, sub), :]
        for sh in (1, 2, 4):
            keep = row >= sh
            a_sh = jnp.where(keep, pltpu.roll(a8, sh, 0), 1.0)
            b_sh = jnp.where(keep, pltpu.roll(b8, sh, 0), 0.0)
            b8 = a8 * b_sh + b8
            a8 = a8 * a_sh
        h8 = b8 + a8 * h_prev
        b_s[pl.ds(r0, sub), :] = h8
        return jnp.broadcast_to(h8[sub - 1:sub, :], (sub, cw_tile))

    carry[...] = lax.fori_loop(0, ts // sub, step, carry[...])
    out_ref[...] = (_gelu_tanh(g_ref[...].astype(F32)) * b_s[...]).astype(BF16)


def _lru_branch(ug, conv_w, conv_b, wa_bd, wx_bd, ba, bx, lam, batch, seq):
    ts = 512
    cw_tile = LRU_WIDTH // 2
    n_c = LRU_WIDTH // cw_tile
    n_t = seq // ts
    nb = cw_tile // (4 * LRU_BLOCK)
    vec = lambda: pl.BlockSpec((1, cw_tile), lambda b, c, t: (0, c))
    return pl.pallas_call(
        functools.partial(_lru_kernel, ts=ts),
        grid=(batch, n_c, n_t),
        in_specs=[
            pl.BlockSpec((ts, cw_tile), lambda b, c, t: (b * n_t + t, c)),
            pl.BlockSpec((ts, cw_tile), lambda b, c, t: (b * n_t + t, n_c + c)),
            pl.BlockSpec((CONV_WIDTH, cw_tile), lambda b, c, t: (0, c)),
            vec(),
            pl.BlockSpec((nb, 4 * LRU_BLOCK, 4 * LRU_BLOCK), lambda b, c, t: (c, 0, 0)),
            pl.BlockSpec((nb, 4 * LRU_BLOCK, 4 * LRU_BLOCK), lambda b, c, t: (c, 0, 0)),
            vec(), vec(), vec(),
        ],
        out_specs=pl.BlockSpec((ts, cw_tile), lambda b, c, t: (b * n_t + t, c)),
        out_shape=jax.ShapeDtypeStruct((batch * seq, LRU_WIDTH), BF16),
        scratch_shapes=[
            pltpu.VMEM((ts + 8, cw_tile), F32),
            pltpu.VMEM((ts, cw_tile), F32),
            pltpu.VMEM((ts, cw_tile), F32),
            pltpu.VMEM((8, cw_tile), F32),
        ],
        compiler_params=_params(("arbitrary", "arbitrary", "arbitrary")),
        name="lru",
    )(ug, ug, conv_w, conv_b, wa_bd, wx_bd, ba, bx, lam)


def _split_bf16(x):
    hi = x.astype(BF16)
    lo = (x - hi.astype(F32)).astype(BF16)
    return hi, lo


def _mix_kernel(x_ref, o1_ref, o2_ref, o3_ref, l1_ref, l2_ref, l3_ref, lru_ref, gl_ref, bm_ref,
                wpa_ref, wpl_ref, wo_ref, n2_ref, expand_ref, rw_hi_ref, rw_lo_ref, rb_ref,
                xo_ref, h2_ref, comb_ref):
    l1, l2, l3 = l1_ref[...], l2_ref[...], l3_ref[...]
    mx = jnp.maximum(jnp.maximum(l1, l2), l3)
    e1, e2, e3 = jnp.exp(l1 - mx), jnp.exp(l2 - mx), jnp.exp(l3 - mx)
    inv = 1.0 / (e1 + e2 + e3)
    attn = None
    for e, o_ref in ((e1, o1_ref), (e2, o2_ref), (e3, o3_ref)):
        hi, lo = _split_bf16(e * inv)
        wide = (jnp.dot(hi, expand_ref[...], preferred_element_type=F32)
                + jnp.dot(lo, expand_ref[...], preferred_element_type=F32))
        term = wide * o_ref[...].astype(F32)
        attn = term if attn is None else attn + term
    pa = jnp.dot(attn.astype(BF16), wpa_ref[...], preferred_element_type=F32)
    pr = jnp.dot(lru_ref[...], wpl_ref[...], preferred_element_type=F32)
    gates = jax.nn.sigmoid(gl_ref[...].astype(F32) + bm_ref[...])
    mixed = gates[:, :D_MODEL] * pa + gates[:, D_MODEL:] * pr
    x = x_ref[...] + jnp.dot(mixed.astype(BF16), wo_ref[...], preferred_element_type=F32)
    xo_ref[...] = x

    ms = jnp.mean(x * x, axis=-1, keepdims=True)
    h2 = x * lax.rsqrt(ms + EPS) * n2_ref[...]
    h_hi, h_lo = _split_bf16(h2)
    h2_ref[...] = h_hi
    logits = (jnp.dot(h_hi, rw_hi_ref[...], preferred_element_type=F32)
              + jnp.dot(h_hi, rw_lo_ref[...], preferred_element_type=F32)
              + jnp.dot(h_lo, rw_hi_ref[...], preferred_element_type=F32)) + rb_ref[...]
    comb_ref[...] = _route(logits)


def _route(logits):
    tm = logits.shape[0]
    lane = lax.broadcasted_iota(jnp.int32, (tm, LANES), 1)
    neg = -jnp.inf
    big = jnp.int32(LANES)
    is_g = (lane >= N_EXPERTS) & (lane < N_EXPERTS + N_GROUPS)
    gl = jnp.where(is_g, logits, neg)
    gmax = jnp.max(gl, axis=1, keepdims=True)
    g_idx = jnp.min(jnp.where(gl == gmax, lane, big), axis=1, keepdims=True) - N_EXPERTS
    g_w = 1.0 / jnp.sum(jnp.exp(gl - gmax), axis=1, keepdims=True)
    lo = g_idx * EXPERTS_PER_GROUP
    in_grp = (lane >= lo) & (lane < lo + EXPERTS_PER_GROUP)
    el = jnp.where(in_grp, logits, neg)
    v1 = jnp.max(el, axis=1, keepdims=True)
    i1 = jnp.min(jnp.where(el == v1, lane, big), axis=1, keepdims=True)
    el2 = jnp.where(lane == i1, neg, el)
    v2 = jnp.max(el2, axis=1, keepdims=True)
    i2 = jnp.min(jnp.where(el2 == v2, lane, big), axis=1, keepdims=True)
    e2 = jnp.exp(v2 - v1)
    w1 = g_w / (1.0 + e2)
    w2 = g_w * e2 / (1.0 + e2)
    return jnp.where(lane == i1, w1, 0.0) + jnp.where(lane == i2, w2, 0.0)


def _mix(x2, o_list, lse_list, lru_out, gl, b_merge, wpa, wpl, wo, norm2, expand, rw_hi, rw_lo, rb):
    t = x2.shape[0]
    tm = 512
    row = lambda width: pl.BlockSpec((tm, width), lambda i: (i, 0))
    full = lambda a: pl.BlockSpec(a.shape, lambda i: (0,) * a.ndim)
    consts = [b_merge, wpa, wpl, wo, norm2, expand, rw_hi, rw_lo, rb]
    return pl.pallas_call(
        _mix_kernel,
        grid=(t // tm,),
        in_specs=[row(D_MODEL)] + [row(GROUP_WIDTH)] * 3 + [row(LANES)] * 3
                 + [row(LRU_WIDTH), row(2 * D_MODEL)] + [full(a) for a in consts],
        out_specs=[row(D_MODEL), row(D_MODEL), row(LANES)],
        out_shape=[
            jax.ShapeDtypeStruct((t, D_MODEL), F32),
            jax.ShapeDtypeStruct((t, D_MODEL), BF16),
            jax.ShapeDtypeStruct((t, LANES), F32),
        ],
        compiler_params=_params(("arbitrary",)),
        name="mix",
    )(x2, *o_list, *lse_list, lru_out, gl, *consts)


def _moe_kernel(x_ref, h_ref, comb_ref, wg_ref, wu_ref, wd_ref, fn_ref, out_ref, *, final_norm):
    e = pl.program_id(1)

    @pl.when(e == 0)
    def _():
        out_ref[...] = x_ref[...]

    h = h_ref[...]
    gate = jnp.dot(h, wg_ref[...], preferred_element_type=F32)
    up = jnp.dot(h, wu_ref[...], preferred_element_type=F32)
    act = (gate * jax.nn.sigmoid(gate) * up).astype(BF16)
    y = jnp.dot(act, wd_ref[...], preferred_element_type=F32)
    lane = lax.broadcasted_iota(jnp.int32, comb_ref.shape, 1)
    w_e = jnp.sum(jnp.where(lane == e, comb_ref[...], 0.0), axis=1, keepdims=True)
    out_ref[...] += w_e * y

    if final_norm:
        @pl.when(e == N_EXPERTS - 1)
        def _():
            x = out_ref[...]
            ms = jnp.mean(x * x, axis=-1, keepdims=True)
            out_ref[...] = x * lax.rsqrt(ms + EPS) * fn_ref[...]


def _moe(x2, h2, comb, wg, wu, wd, final_scale, final_norm):
    t = x2.shape[0]
    tm = 1024
    return pl.pallas_call(
        functools.partial(_moe_kernel, final_norm=final_norm),
        grid=(t // tm, N_EXPERTS),
        in_specs=[
            pl.BlockSpec((tm, D_MODEL), lambda i, e: (i, 0)),
            pl.BlockSpec((tm, D_MODEL), lambda i, e: (i, 0)),
            pl.BlockSpec((tm, LANES), lambda i, e: (i, 0)),
            pl.BlockSpec((None, D_MODEL, D_EXPERT), lambda i, e: (e, 0, 0)),
            pl.BlockSpec((None, D_MODEL, D_EXPERT), lambda i, e: (e, 0, 0)),
            pl.BlockSpec((None, D_EXPERT, D_MODEL), lambda i, e: (e, 0, 0)),
            pl.BlockSpec((1, D_MODEL), lambda i, e: (0, 0)),
        ],
        out_specs=pl.BlockSpec((tm, D_MODEL), lambda i, e: (i, 0)),
        out_shape=jax.ShapeDtypeStruct((t, D_MODEL), F32),
        compiler_params=_params(("arbitrary", "arbitrary")),
        name="moe",
    )(x2, h2, comb, wg, wu, wd, final_scale)


def _rotary_tables(seq):
    inv = ROPE_THETA ** (-jnp.arange(0, HEAD_DIM, 2, dtype=F32) / HEAD_DIM)
    ang = jnp.arange(seq, dtype=F32)[:, None] * inv[None, :]
    cos, sin = jnp.cos(ang), jnp.sin(ang)
    return jnp.tile(cos, (1, 4)), jnp.concatenate([-sin, sin, -sin, sin], axis=1)


def _block_diag(w):
    nb = w.shape[0] // 4
    w4 = w.reshape(nb, 4, LRU_BLOCK, LRU_BLOCK)
    eye = jnp.eye(4, dtype=w.dtype)
    return jnp.einsum('qaij,ab->qaibj', w4, eye).reshape(nb, 4 * LRU_BLOCK, 4 * LRU_BLOCK)


def kernel(x, norm1_scale, w_in, b_merge, conv_w, conv_b, lru_wa, lru_ba, lru_wx, lru_bx, lru_lambda,
           w_proj_attn, w_proj_lru, w_out, norm2_scale, router_group_w, router_group_b, router_expert_w,
           router_expert_b, expert_w_gate, expert_w_up, expert_w_down, final_norm_scale):
    batch, seq, _ = x.shape
    depth = w_in.shape[0]
    cos_t, sin_t = _rotary_tables(seq)
    expand = (jnp.arange(LANES)[:, None] == (jnp.arange(GROUP_WIDTH)[None, :] // HEAD_DIM)).astype(BF16)
    row = lambda v: v.reshape(1, -1)
    x2 = x.reshape(batch * seq, D_MODEL)
    for l in range(depth):
        qkv, ug, gl = _in_proj(x2, row(norm1_scale[l]), w_in[l].astype(BF16), cos_t, sin_t, seq)
        o_list, lse_list = [], []
        for g in range(len(DILATIONS)):
            o, lse = _attention_group(qkv, batch, seq, g)
            o_list.append(o)
            lse_list.append(lse)
        lru_out = _lru_branch(ug, conv_w[l], row(conv_b[l]), _block_diag(lru_wa[l]).astype(BF16),
                              _block_diag(lru_wx[l]).astype(BF16), row(lru_ba[l]), row(lru_bx[l]),
                              row(lru_lambda[l]), batch, seq)
        rw = jnp.zeros((D_MODEL, LANES), F32)
        rw = rw.at[:, :N_EXPERTS].set(router_expert_w[l]).at[:, N_EXPERTS:N_EXPERTS + N_GROUPS].set(
            router_group_w[l])
        rb = jnp.zeros((1, LANES), F32)
        rb = rb.at[0, :N_EXPERTS].set(router_expert_b[l]).at[0, N_EXPERTS:N_EXPERTS + N_GROUPS].set(
            router_group_b[l])
        rw_hi = rw.astype(BF16)
        rw_lo = (rw - rw_hi.astype(F32)).astype(BF16)
        x_mid, h2, comb = _mix(x2, o_list, lse_list, lru_out, gl, row(b_merge[l]),
                               w_proj_attn[l].astype(BF16), w_proj_lru[l].astype(BF16),
                               w_out[l].astype(BF16), row(norm2_scale[l]), expand, rw_hi, rw_lo, rb)
        x2 = _moe(x_mid, h2, comb, expert_w_gate[l].astype(BF16), expert_w_up[l].astype(BF16),
                  expert_w_down[l].astype(BF16), row(final_norm_scale), final_norm=(l == depth - 1))
    return x2.reshape(batch, seq, D_MODEL)
```

```python
import functools

import jax
import jax.numpy as jnp
from jax import lax
from jax.experimental import pallas as pl
from jax.experimental.pallas import tpu as pltpu

D_MODEL = 1024
HEAD_DIM = 64
GROUP_WIDTH = 512
DILATIONS = (1, 4, 16)
WINDOW_STEPS = 128
ATTN_WIDTH = 3 * GROUP_WIDTH
LRU_WIDTH = 1024
LRU_BLOCK = 64
N_IN = 3 * ATTN_WIDTH + 2 * LRU_WIDTH + 2 * D_MODEL
CONV_WIDTH = 4
LRU_C = 8.0
N_GROUPS = 4
EXPERTS_PER_GROUP = 4
N_EXPERTS = 16
D_EXPERT = 512
EPS = 1e-6
ROPE_THETA = 10000.0
LANES = 128
VMEM_LIMIT = 56 * 1024 * 1024
IN_PROJ_TILE = 512

F32 = jnp.float32
BF16 = jnp.bfloat16


def _params(semantics):
    return pltpu.CompilerParams(dimension_semantics=semantics, vmem_limit_bytes=VMEM_LIMIT)


def _in_proj_kernel(x_ref, scale_ref, w_ref, cos_ref, sin_ref, *rest):
    qkv_refs, (ug_ref, gl_ref, h32_ref, hn_ref) = rest[:9], rest[9:]
    x = x_ref[...]
    tm = x.shape[0]
    ms = jnp.mean(x * x, axis=-1, keepdims=True)
    hn = x * lax.rsqrt(ms + EPS) * scale_ref[...]
    hn_ref[0] = hn.astype(BF16)
    n_lane_blocks = D_MODEL // LANES
    for j in range(n_lane_blocks):
        h32_ref[j] = hn[:, j * LANES:(j + 1) * LANES]
    for g in (1, 2):
        d = DILATIONS[g]
        rows = tm // d
        for r in range(d):
            for j in range(n_lane_blocks):
                hn_ref[g, r * rows:(r + 1) * rows, j * LANES:(j + 1) * LANES] = (
                    h32_ref[j, pl.ds(r, rows, stride=d), :].astype(BF16))
    lane = lax.broadcasted_iota(jnp.int32, (tm, LANES), 1)
    first_half = (lane % HEAD_DIM) < (HEAD_DIM // 2)
    n_chunks = N_IN // GROUP_WIDTH
    for c in range(n_chunks):
        g = c % 3 if c < 9 else 0
        res = jnp.dot(hn_ref[g], w_ref[:, c * GROUP_WIDTH:(c + 1) * GROUP_WIDTH],
                      preferred_element_type=F32)
        if c < 6:
            cos = cos_ref[g]
            sin = sin_ref[g]
            parts = []
            for s in range(GROUP_WIDTH // LANES):
                blk = res[:, s * LANES:(s + 1) * LANES]
                partner = jnp.where(first_half, pltpu.roll(blk, LANES - HEAD_DIM // 2, 1),
                                    pltpu.roll(blk, HEAD_DIM // 2, 1))
                o = blk * cos + partner * sin
                if c < 3:
                    o = o * (HEAD_DIM ** -0.5)
                parts.append(o)
            res = jnp.concatenate(parts, axis=1)
        if c < 9:
            d = DILATIONS[g]
            qkv_refs[c][...] = res.reshape(d, tm // d, GROUP_WIDTH).astype(BF16)
        elif c < 13:
            ug_ref[:, (c - 9) * GROUP_WIDTH:(c - 8) * GROUP_WIDTH] = res.astype(BF16)
        else:
            gl_ref[:, (c - 13) * GROUP_WIDTH:(c - 12) * GROUP_WIDTH] = res.astype(BF16)


def _in_proj(x2, scale, w_bf, cos_t, sin_t, batch, seq):
    t = x2.shape[0]
    tm = IN_PROJ_TILE
    n_seq_tiles = seq // tm
    qkv_specs, qkv_shapes = [], []
    for c in range(9):
        d = DILATIONS[c % 3]
        qkv_specs.append(pl.BlockSpec((None, d, tm // d, GROUP_WIDTH),
                                      lambda i: (i // n_seq_tiles, 0, i % n_seq_tiles, 0)))
        qkv_shapes.append(jax.ShapeDtypeStruct((batch, d, seq // d, GROUP_WIDTH), BF16))
    return pl.pallas_call(
        _in_proj_kernel,
        grid=(t // tm,),
        in_specs=[
            pl.BlockSpec((tm, D_MODEL), lambda i: (i, 0)),
            pl.BlockSpec((1, D_MODEL), lambda i: (0, 0)),
            pl.BlockSpec((D_MODEL, N_IN), lambda i: (0, 0), pipeline_mode=pl.Buffered(1)),
            pl.BlockSpec((3, tm, LANES), lambda i: (0, i % n_seq_tiles, 0)),
            pl.BlockSpec((3, tm, LANES), lambda i: (0, i % n_seq_tiles, 0)),
        ],
        out_specs=qkv_specs + [
            pl.BlockSpec((tm, 2 * LRU_WIDTH), lambda i: (i, 0)),
            pl.BlockSpec((tm, 2 * D_MODEL), lambda i: (i, 0)),
        ],
        out_shape=qkv_shapes + [
            jax.ShapeDtypeStruct((t, 2 * LRU_WIDTH), BF16),
            jax.ShapeDtypeStruct((t, 2 * D_MODEL), BF16),
        ],
        scratch_shapes=[pltpu.VMEM((D_MODEL // LANES, tm, LANES), F32), pltpu.VMEM((3, tm, D_MODEL), BF16)],
        compiler_params=_params(("arbitrary",)),
        name="in_proj",
    )(x2, scale, w_bf, cos_t, sin_t)


def _attn_kernel(q_ref, kp_ref, kc_ref, vp_ref, vc_ref, o_ref, lse_ref, kcat, vcat, *, tq):
    w = WINDOW_STEPS
    kcat[0:w, :] = kp_ref[...]
    kcat[w:, :] = kc_ref[...]
    vcat[0:w, :] = vp_ref[...]
    vcat[w:, :] = vc_ref[...]
    first_tile = pl.program_id(2) == 0

    qi = lax.broadcasted_iota(jnp.int32, (2 * w, 2 * w), 0) % w
    col = lax.broadcasted_iota(jnp.int32, (2 * w, 2 * w), 1)
    band = (col >= qi) & (col <= qi + w)
    lane_q = lax.broadcasted_iota(jnp.int32, (w, LANES), 1)
    low_q = lane_q < HEAD_DIM
    lane_l = lax.broadcasted_iota(jnp.int32, (w, LANES), 1)

    def sub_tile(s, carry):
        r0 = pl.multiple_of(s * w, w)
        min_col = jnp.where(first_tile & (s == 0), w, 0)
        valid = band & (col >= min_col)
        lse_tile = jnp.zeros((w, LANES), F32)
        for p in range(GROUP_WIDTH // LANES):
            lanes = slice(p * LANES, (p + 1) * LANES)
            qp = q_ref[pl.ds(r0, w), lanes]
            zero = jnp.zeros_like(qp)
            qs = jnp.concatenate([jnp.where(low_q, qp, zero), jnp.where(low_q, zero, qp)], axis=0)
            kk = kcat[pl.ds(r0, 2 * w), lanes]
            vv = vcat[pl.ds(r0, 2 * w), lanes]
            sc = lax.dot_general(qs, kk, (((1,), (1,)), ((), ())), preferred_element_type=F32)
            sc = jnp.where(valid, sc, -jnp.inf)
            m = jnp.max(sc, axis=1, keepdims=True)
            pr = jnp.exp(sc - m)
            den = jnp.sum(pr, axis=1, keepdims=True)
            pv = jnp.dot(pr.astype(BF16), vv, preferred_element_type=F32) / den
            o_ref[pl.ds(r0, w), lanes] = jnp.where(low_q, pv[:w], pv[w:]).astype(BF16)
            lse = m + jnp.log(den)
            lse_tile = jnp.where(lane_l == 2 * p, lse[:w], lse_tile)
            lse_tile = jnp.where(lane_l == 2 * p + 1, lse[w:], lse_tile)
        lse_ref[pl.ds(r0, w), :] = lse_tile
        return carry

    lax.fori_loop(0, tq // w, sub_tile, 0)


def _attention_group(q, k, v, group):
    batch, d, length, _ = q.shape
    tq = 512
    w = WINDOW_STEPS
    sub = tq // w
    cur = pl.BlockSpec((None, None, tq, GROUP_WIDTH), lambda b, r, i: (b, r, i, 0))
    prev = pl.BlockSpec((None, None, w, GROUP_WIDTH),
                        lambda b, r, i: (b, r, jnp.maximum(i * sub - 1, 0), 0))
    return pl.pallas_call(
        functools.partial(_attn_kernel, tq=tq),
        grid=(batch, d, length // tq),
        in_specs=[cur, prev, cur, prev, cur],
        out_specs=[
            pl.BlockSpec((None, None, tq, GROUP_WIDTH), lambda b, r, i: (b, r, i, 0)),
            pl.BlockSpec((None, None, tq, LANES), lambda b, r, i: (b, r, i, 0)),
        ],
        out_shape=[
            jax.ShapeDtypeStruct((batch, d, length, GROUP_WIDTH), BF16),
            jax.ShapeDtypeStruct((batch, d, length, LANES), F32),
        ],
        scratch_shapes=[pltpu.VMEM((tq + w, GROUP_WIDTH), BF16), pltpu.VMEM((tq + w, GROUP_WIDTH), BF16)],
        compiler_params=_params(("arbitrary", "arbitrary", "arbitrary")),
        name=f"attn_d{d}",
    )(q, k, k, v, v)


def _gelu_tanh(x):
    return 0.5 * x * (1.0 + jnp.tanh(0.7978845608028654 * (x + 0.044715 * (x * x * x))))


def _lru_kernel(u_ref, g_ref, cw_ref, cb_ref, wa_ref, wx_ref, ba_ref, bx_ref, lam_ref, out_ref,
                ext, a_s, b_s, carry, *, ts):
    cw_tile = LRU_WIDTH // 2
    sub = 8

    @pl.when(pl.program_id(2) == 0)
    def _():
        ext[0:sub, :] = jnp.zeros((sub, cw_tile), F32)
        carry[...] = jnp.zeros((sub, cw_tile), F32)

    u = u_ref[...].astype(F32)
    ext[sub:, :] = u
    conv = cb_ref[...] + cw_ref[0:1, :] * ext[sub - 3:sub - 3 + ts, :]
    for j in range(1, CONV_WIDTH):
        conv = conv + cw_ref[j:j + 1, :] * ext[sub - 3 + j:sub - 3 + j + ts, :]
    ext[0:sub, :] = u[ts - sub:, :]

    xb = conv.astype(BF16)
    blk = 4 * LRU_BLOCK
    ra = jnp.concatenate(
        [jnp.dot(xb[:, q * blk:(q + 1) * blk], wa_ref[q], preferred_element_type=F32)
         for q in range(cw_tile // blk)], axis=1)
    rx = jnp.concatenate(
        [jnp.dot(xb[:, q * blk:(q + 1) * blk], wx_ref[q], preferred_element_type=F32)
         for q in range(cw_tile // blk)], axis=1)
    r = jax.nn.sigmoid(ra + ba_ref[...])
    gate_i = jax.nn.sigmoid(rx + bx_ref[...])
    z = -lam_ref[...]
    softplus = jnp.maximum(z, 0.0) + jnp.log1p(jnp.exp(-jnp.abs(z)))
    log_a = (-LRU_C) * r * softplus
    a = jnp.exp(log_a)
    a_s[...] = a
    b_s[...] = jnp.sqrt(1.0 - a * a) * (gate_i * conv)

    row = lax.broadcasted_iota(jnp.int32, (sub, cw_tile), 0)

    def step(k, h_prev):
        r0 = pl.multiple_of(k * sub, sub)
        a8 = a_s[pl.ds(r0, sub), :]
        b8 = b_s[pl.ds(r0, sub), :]
        for sh in (1, 2, 4):
            keep = row >= sh
            a_sh = jnp.where(keep, pltpu.roll(a8, sh, 0), 1.0)
            b_sh = jnp.where(keep, pltpu.roll(b8, sh, 0), 0.0)
            b8 = a8 * b_sh + b8
            a8 = a8 * a_sh
        h8 = b8 + a8 * h_prev
        b_s[pl.ds(r0, sub), :] = h8
        return jnp.broadcast_to(h8[sub - 1:sub, :], (sub, cw_tile))

    carry[...] = lax.fori_loop(0, ts // sub, step, carry[...])
    out_ref[...] = (_gelu_tanh(g_ref[...].astype(F32)) * b_s[...]).astype(BF16)


def _lru_branch(ug, conv_w, conv_b, wa_bd, wx_bd, ba, bx, lam, batch, seq):
    ts = 512
    cw_tile = LRU_WIDTH // 2
    n_c = LRU_WIDTH // cw_tile
    n_t = seq // ts
    nb = cw_tile // (4 * LRU_BLOCK)
    vec = lambda: pl.BlockSpec((1, cw_tile), lambda b, c, t: (0, c))
    return pl.pallas_call(
        functools.partial(_lru_kernel, ts=ts),
        grid=(batch, n_c, n_t),
        in_specs=[
            pl.BlockSpec((ts, cw_tile), lambda b, c, t: (b * n_t + t, c)),
            pl.BlockSpec((ts, cw_tile), lambda b, c, t: (b * n_t + t, n_c + c)),
            pl.BlockSpec((CONV_WIDTH, cw_tile), lambda b, c, t: (0, c)),
            vec(),
            pl.BlockSpec((nb, 4 * LRU_BLOCK, 4 * LRU_BLOCK), lambda b, c, t: (c, 0, 0)),
            pl.BlockSpec((nb, 4 * LRU_BLOCK, 4 * LRU_BLOCK), lambda b, c, t: (c, 0, 0)),
            vec(), vec(), vec(),
        ],
        out_specs=pl.BlockSpec((ts, cw_tile), lambda b, c, t: (b * n_t + t, c)),
        out_shape=jax.ShapeDtypeStruct((batch * seq, LRU_WIDTH), BF16),
        scratch_shapes=[
            pltpu.VMEM((ts + 8, cw_tile), F32),
            pltpu.VMEM((ts, cw_tile), F32),
            pltpu.VMEM((ts, cw_tile), F32),
            pltpu.VMEM((8, cw_tile), F32),
        ],
        compiler_params=_params(("arbitrary", "arbitrary", "arbitrary")),
        name="lru",
    )(ug, ug, conv_w, conv_b, wa_bd, wx_bd, ba, bx, lam)


def _split_bf16(x):
    hi = x.astype(BF16)
    lo = (x - hi.astype(F32)).astype(BF16)
    return hi, lo


def _mix_kernel(x_ref, o1_ref, o2_ref, o3_ref, l1_ref, l2_ref, l3_ref, lru_ref, gl_ref, bm_ref,
                wpa_ref, wpl_ref, wo_ref, n2_ref, expand_ref, rw_hi_ref, rw_lo_ref, rb_ref,
                xo_ref, h2_ref, comb_ref, o2_s, o3_s, l2_s, l3_s):
    tm = x_ref.shape[0]
    for src_o, src_l, dst_o, dst_l in ((o2_ref, l2_ref, o2_s, l2_s), (o3_ref, l3_ref, o3_s, l3_s)):
        d = src_o.shape[0]
        for r in range(d):
            o_r = src_o[r].astype(F32)
            for j in range(GROUP_WIDTH // LANES):
                dst_o[j, pl.ds(r, tm // d, stride=d), :] = o_r[:, j * LANES:(j + 1) * LANES]
            dst_l[pl.ds(r, tm // d, stride=d), :] = src_l[r]
    tok = lambda s: jnp.concatenate([s[j] for j in range(GROUP_WIDTH // LANES)], axis=1)
    l1, l2, l3 = l1_ref[0], l2_s[...], l3_s[...]
    mx = jnp.maximum(jnp.maximum(l1, l2), l3)
    e1, e2, e3 = jnp.exp(l1 - mx), jnp.exp(l2 - mx), jnp.exp(l3 - mx)
    inv = 1.0 / (e1 + e2 + e3)
    attn = None
    for e, o_tok in ((e1, o1_ref[0].astype(F32)), (e2, tok(o2_s)), (e3, tok(o3_s))):
        hi, lo = _split_bf16(e * inv)
        wide = (jnp.dot(hi, expand_ref[...], preferred_element_type=F32)
                + jnp.dot(lo, expand_ref[...], preferred_element_type=F32))
        term = wide * o_tok
        attn = term if attn is None else attn + term
    pa = jnp.dot(attn.astype(BF16), wpa_ref[...], preferred_element_type=F32)
    pr = jnp.dot(lru_ref[...], wpl_ref[...], preferred_element_type=F32)
    gates = jax.nn.sigmoid(gl_ref[...].astype(F32) + bm_ref[...])
    mixed = gates[:, :D_MODEL] * pa + gates[:, D_MODEL:] * pr
    x = x_ref[...] + jnp.dot(mixed.astype(BF16), wo_ref[...], preferred_element_type=F32)
    xo_ref[...] = x

    ms = jnp.mean(x * x, axis=-1, keepdims=True)
    h2 = x * lax.rsqrt(ms + EPS) * n2_ref[...]
    h_hi, h_lo = _split_bf16(h2)
    h2_ref[...] = h_hi
    logits = (jnp.dot(h_hi, rw_hi_ref[...], preferred_element_type=F32)
              + jnp.dot(h_hi, rw_lo_ref[...], preferred_element_type=F32)
              + jnp.dot(h_lo, rw_hi_ref[...], preferred_element_type=F32)) + rb_ref[...]
    comb_ref[...] = _route(logits)


def _route(logits):
    tm = logits.shape[0]
    lane = lax.broadcasted_iota(jnp.int32, (tm, LANES), 1)
    neg = -jnp.inf
    big = jnp.int32(LANES)
    is_g = (lane >= N_EXPERTS) & (lane < N_EXPERTS + N_GROUPS)
    gl = jnp.where(is_g, logits, neg)
    gmax = jnp.max(gl, axis=1, keepdims=True)
    g_idx = jnp.min(jnp.where(gl == gmax, lane, big), axis=1, keepdims=True) - N_EXPERTS
    g_w = 1.0 / jnp.sum(jnp.exp(gl - gmax), axis=1, keepdims=True)
    lo = g_idx * EXPERTS_PER_GROUP
    in_grp = (lane >= lo) & (lane < lo + EXPERTS_PER_GROUP)
    el = jnp.where(in_grp, logits, neg)
    v1 = jnp.max(el, axis=1, keepdims=True)
    i1 = jnp.min(jnp.where(el == v1, lane, big), axis=1, keepdims=True)
    el2 = jnp.where(lane == i1, neg, el)
    v2 = jnp.max(el2, axis=1, keepdims=True)
    i2 = jnp.min(jnp.where(el2 == v2, lane, big), axis=1, keepdims=True)
    e2 = jnp.exp(v2 - v1)
    w1 = g_w / (1.0 + e2)
    w2 = g_w * e2 / (1.0 + e2)
    return jnp.where(lane == i1, w1, 0.0) + jnp.where(lane == i2, w2, 0.0)


def _mix(x2, o_list, lse_list, lru_out, gl, b_merge, wpa, wpl, wo, norm2, expand, rw_hi, rw_lo, rb, seq):
    t = x2.shape[0]
    tm = IN_PROJ_TILE
    n_seq_tiles = seq // tm
    row = lambda width: pl.BlockSpec((tm, width), lambda i: (i, 0))
    full = lambda a: pl.BlockSpec(a.shape, lambda i: (0,) * a.ndim)
    resid = lambda d, width: pl.BlockSpec((None, d, tm // d, width),
                                          lambda i: (i // n_seq_tiles, 0, i % n_seq_tiles, 0))
    consts = [b_merge, wpa, wpl, wo, norm2, expand, rw_hi, rw_lo, rb]
    return pl.pallas_call(
        _mix_kernel,
        grid=(t // tm,),
        in_specs=[row(D_MODEL)] + [resid(d, GROUP_WIDTH) for d in DILATIONS]
                 + [resid(d, LANES) for d in DILATIONS]
                 + [row(LRU_WIDTH), row(2 * D_MODEL)] + [full(a) for a in consts],
        out_specs=[row(D_MODEL), row(D_MODEL), row(LANES)],
        out_shape=[
            jax.ShapeDtypeStruct((t, D_MODEL), F32),
            jax.ShapeDtypeStruct((t, D_MODEL), BF16),
            jax.ShapeDtypeStruct((t, LANES), F32),
        ],
        scratch_shapes=[pltpu.VMEM((GROUP_WIDTH // LANES, tm, LANES), F32),
                        pltpu.VMEM((GROUP_WIDTH // LANES, tm, LANES), F32),
                        pltpu.VMEM((tm, LANES), F32), pltpu.VMEM((tm, LANES), F32)],
        compiler_params=_params(("arbitrary",)),
        name="mix",
    )(x2, *o_list, *lse_list, lru_out, gl, *consts)


def _moe_kernel(x_ref, h_ref, comb_ref, wg_ref, wu_ref, wd_ref, fn_ref, out_ref, *, final_norm):
    e = pl.program_id(1)

    @pl.when(e == 0)
    def _():
        out_ref[...] = x_ref[...]

    h = h_ref[...]
    gate = jnp.dot(h, wg_ref[...], preferred_element_type=F32)
    up = jnp.dot(h, wu_ref[...], preferred_element_type=F32)
    act = (gate * jax.nn.sigmoid(gate) * up).astype(BF16)
    y = jnp.dot(act, wd_ref[...], preferred_element_type=F32)
    lane = lax.broadcasted_iota(jnp.int32, comb_ref.shape, 1)
    w_e = jnp.sum(jnp.where(lane == e, comb_ref[...], 0.0), axis=1, keepdims=True)
    out_ref[...] += w_e * y

    if final_norm:
        @pl.when(e == N_EXPERTS - 1)
        def _():
            x = out_ref[...]
            ms = jnp.mean(x * x, axis=-1, keepdims=True)
            out_ref[...] = x * lax.rsqrt(ms + EPS) * fn_ref[...]


def _moe(x2, h2, comb, wg, wu, wd, final_scale, final_norm):
    t = x2.shape[0]
    tm = 1024
    return pl.pallas_call(
        functools.partial(_moe_kernel, final_norm=final_norm),
        grid=(t // tm, N_EXPERTS),
        in_specs=[
            pl.BlockSpec((tm, D_MODEL), lambda i, e: (i, 0)),
            pl.BlockSpec((tm, D_MODEL), lambda i, e: (i, 0)),
            pl.BlockSpec((tm, LANES), lambda i, e: (i, 0)),
            pl.BlockSpec((None, D_MODEL, D_EXPERT), lambda i, e: (e, 0, 0)),
            pl.BlockSpec((None, D_MODEL, D_EXPERT), lambda i, e: (e, 0, 0)),
            pl.BlockSpec((None, D_EXPERT, D_MODEL), lambda i, e: (e, 0, 0)),
            pl.BlockSpec((1, D_MODEL), lambda i, e: (0, 0)),
        ],
        out_specs=pl.BlockSpec((tm, D_MODEL), lambda i, e: (i, 0)),
        out_shape=jax.ShapeDtypeStruct((t, D_MODEL), F32),
        compiler_params=_params(("arbitrary", "arbitrary")),
        name="moe",
    )(x2, h2, comb, wg, wu, wd, final_scale)


def _rotary_tables(seq):
    inv = ROPE_THETA ** (-jnp.arange(0, HEAD_DIM, 2, dtype=F32) / HEAD_DIM)
    ang = jnp.arange(seq, dtype=F32)[:, None] * inv[None, :]
    cos, sin = jnp.cos(ang), jnp.sin(ang)
    cos_t, sin_t = jnp.tile(cos, (1, 4)), jnp.concatenate([-sin, sin, -sin, sin], axis=1)

    def residue_major(tab):
        out = []
        for d in DILATIONS:
            tiles = tab.reshape(seq // IN_PROJ_TILE, IN_PROJ_TILE // d, d, LANES)
            out.append(tiles.transpose(0, 2, 1, 3).reshape(seq, LANES))
        return jnp.stack(out)

    return residue_major(cos_t), residue_major(sin_t)


def _block_diag(w):
    nb = w.shape[0] // 4
    w4 = w.reshape(nb, 4, LRU_BLOCK, LRU_BLOCK)
    eye = jnp.eye(4, dtype=w.dtype)
    return jnp.einsum('qaij,ab->qaibj', w4, eye).reshape(nb, 4 * LRU_BLOCK, 4 * LRU_BLOCK)


def kernel(x, norm1_scale, w_in, b_merge, conv_w, conv_b, lru_wa, lru_ba, lru_wx, lru_bx, lru_lambda,
           w_proj_attn, w_proj_lru, w_out, norm2_scale, router_group_w, router_group_b, router_expert_w,
           router_expert_b, expert_w_gate, expert_w_up, expert_w_down, final_norm_scale):
    batch, seq, _ = x.shape
    depth = w_in.shape[0]
    cos_t, sin_t = _rotary_tables(seq)
    expand = (jnp.arange(LANES)[:, None] == (jnp.arange(GROUP_WIDTH)[None, :] // HEAD_DIM)).astype(BF16)
    row = lambda v: v.reshape(1, -1)
    x2 = x.reshape(batch * seq, D_MODEL)
    for l in range(depth):
        *qkv, ug, gl = _in_proj(x2, row(norm1_scale[l]), w_in[l].astype(BF16), cos_t, sin_t, batch, seq)
        o_list, lse_list = [], []
        for g in range(len(DILATIONS)):
            o, lse = _attention_group(qkv[g], qkv[3 + g], qkv[6 + g], g)
            o_list.append(o)
            lse_list.append(lse)
        lru_out = _lru_branch(ug, conv_w[l], row(conv_b[l]), _block_diag(lru_wa[l]).astype(BF16),
                              _block_diag(lru_wx[l]).astype(BF16), row(lru_ba[l]), row(lru_bx[l]),
                              row(lru_lambda[l]), batch, seq)
        rw = jnp.zeros((D_MODEL, LANES), F32)
        rw = rw.at[:, :N_EXPERTS].set(router_expert_w[l]).at[:, N_EXPERTS:N_EXPERTS + N_GROUPS].set(
            router_group_w[l])
        rb = jnp.zeros((1, LANES), F32)
        rb = rb.at[0, :N_EXPERTS].set(router_expert_b[l]).at[0, N_EXPERTS:N_EXPERTS + N_GROUPS].set(
            router_group_b[l])
        rw_hi = rw.astype(BF16)
        rw_lo = (rw - rw_hi.astype(F32)).astype(BF16)
        x_mid, h2, comb = _mix(x2, o_list, lse_list, lru_out, gl, row(b_merge[l]),
                               w_proj_attn[l].astype(BF16), w_proj_lru[l].astype(BF16),
                               w_out[l].astype(BF16), row(norm2_scale[l]), expand, rw_hi, rw_lo, rb, seq)
        x2 = _moe(x_mid, h2, comb, expert_w_gate[l].astype(BF16), expert_w_up[l].astype(BF16),
                  expert_w_down[l].astype(BF16), row(final_norm_scale), final_norm=(l == depth - 1))
    return x2.reshape(batch, seq, D_MODEL)
```

```python
import functools

import jax
import jax.numpy as jnp
from jax import lax
from jax.experimental import pallas as pl
from jax.experimental.pallas import tpu as pltpu

D_MODEL = 1024
HEAD_DIM = 64
GROUP_WIDTH = 512
DILATIONS = (1, 4, 16)
WINDOW_STEPS = 128
ATTN_WIDTH = 3 * GROUP_WIDTH
LRU_WIDTH = 1024
LRU_BLOCK = 64
N_IN = 3 * ATTN_WIDTH + 2 * LRU_WIDTH + 2 * D_MODEL
CONV_WIDTH = 4
LRU_C = 8.0
N_GROUPS = 4
EXPERTS_PER_GROUP = 4
N_EXPERTS = 16
D_EXPERT = 512
EPS = 1e-6
ROPE_THETA = 10000.0
LANES = 128
VMEM_LIMIT = 56 * 1024 * 1024
IN_PROJ_TILE = 512
MOE_ROW = D_MODEL + LANES
MOE_TOKEN_TILE = 512
MOE_GROUP_TILE = 512

F32 = jnp.float32
BF16 = jnp.bfloat16


def _params(semantics):
    return pltpu.CompilerParams(dimension_semantics=semantics, vmem_limit_bytes=VMEM_LIMIT)


def _in_proj_kernel(x_ref, scale_ref, w_ref, cos_ref, sin_ref, *rest):
    qkv_refs, (ug_ref, gl_ref, h32_ref, hn_ref) = rest[:9], rest[9:]
    x = x_ref[...]
    tm = x.shape[0]
    ms = jnp.mean(x * x, axis=-1, keepdims=True)
    hn = x * lax.rsqrt(ms + EPS) * scale_ref[...]
    hn_ref[0] = hn.astype(BF16)
    n_lane_blocks = D_MODEL // LANES
    for j in range(n_lane_blocks):
        h32_ref[j] = hn[:, j * LANES:(j + 1) * LANES]
    for g in (1, 2):
        d = DILATIONS[g]
        rows = tm // d
        for r in range(d):
            for j in range(n_lane_blocks):
                hn_ref[g, r * rows:(r + 1) * rows, j * LANES:(j + 1) * LANES] = (
                    h32_ref[j, pl.ds(r, rows, stride=d), :].astype(BF16))
    lane = lax.broadcasted_iota(jnp.int32, (tm, LANES), 1)
    first_half = (lane % HEAD_DIM) < (HEAD_DIM // 2)
    n_chunks = N_IN // GROUP_WIDTH
    for c in range(n_chunks):
        g = c % 3 if c < 9 else 0
        res = jnp.dot(hn_ref[g], w_ref[:, c * GROUP_WIDTH:(c + 1) * GROUP_WIDTH],
                      preferred_element_type=F32)
        if c < 6:
            cos = cos_ref[g]
            sin = sin_ref[g]
            parts = []
            for s in range(GROUP_WIDTH // LANES):
                blk = res[:, s * LANES:(s + 1) * LANES]
                partner = jnp.where(first_half, pltpu.roll(blk, LANES - HEAD_DIM // 2, 1),
                                    pltpu.roll(blk, HEAD_DIM // 2, 1))
                o = blk * cos + partner * sin
                if c < 3:
                    o = o * (HEAD_DIM ** -0.5)
                parts.append(o)
            res = jnp.concatenate(parts, axis=1)
        if c < 9:
            d = DILATIONS[g]
            qkv_refs[c][...] = res.reshape(d, tm // d, GROUP_WIDTH).astype(BF16)
        elif c < 13:
            ug_ref[:, (c - 9) * GROUP_WIDTH:(c - 8) * GROUP_WIDTH] = res.astype(BF16)
        else:
            gl_ref[:, (c - 13) * GROUP_WIDTH:(c - 12) * GROUP_WIDTH] = res.astype(BF16)


def _in_proj(x2, scale, w_bf, cos_t, sin_t, batch, seq):
    t = x2.shape[0]
    tm = IN_PROJ_TILE
    n_seq_tiles = seq // tm
    qkv_specs, qkv_shapes = [], []
    for c in range(9):
        d = DILATIONS[c % 3]
        qkv_specs.append(pl.BlockSpec((None, d, tm // d, GROUP_WIDTH),
                                      lambda i: (i // n_seq_tiles, 0, i % n_seq_tiles, 0)))
        qkv_shapes.append(jax.ShapeDtypeStruct((batch, d, seq // d, GROUP_WIDTH), BF16))
    return pl.pallas_call(
        _in_proj_kernel,
        grid=(t // tm,),
        in_specs=[
            pl.BlockSpec((tm, D_MODEL), lambda i: (i, 0)),
            pl.BlockSpec((1, D_MODEL), lambda i: (0, 0)),
            pl.BlockSpec((D_MODEL, N_IN), lambda i: (0, 0), pipeline_mode=pl.Buffered(1)),
            pl.BlockSpec((3, tm, LANES), lambda i: (0, i % n_seq_tiles, 0)),
            pl.BlockSpec((3, tm, LANES), lambda i: (0, i % n_seq_tiles, 0)),
        ],
        out_specs=qkv_specs + [
            pl.BlockSpec((tm, 2 * LRU_WIDTH), lambda i: (i, 0)),
            pl.BlockSpec((tm, 2 * D_MODEL), lambda i: (i, 0)),
        ],
        out_shape=qkv_shapes + [
            jax.ShapeDtypeStruct((t, 2 * LRU_WIDTH), BF16),
            jax.ShapeDtypeStruct((t, 2 * D_MODEL), BF16),
        ],
        scratch_shapes=[pltpu.VMEM((D_MODEL // LANES, tm, LANES), F32), pltpu.VMEM((3, tm, D_MODEL), BF16)],
        compiler_params=_params(("arbitrary",)),
        name="in_proj",
    )(x2, scale, w_bf, cos_t, sin_t)


def _attn_kernel(q_ref, kp_ref, kc_ref, vp_ref, vc_ref, o_ref, lse_ref, kcat, vcat, *, tq):
    w = WINDOW_STEPS
    kcat[0:w, :] = kp_ref[...]
    kcat[w:, :] = kc_ref[...]
    vcat[0:w, :] = vp_ref[...]
    vcat[w:, :] = vc_ref[...]
    first_tile = pl.program_id(2) == 0

    qi = lax.broadcasted_iota(jnp.int32, (2 * w, 2 * w), 0) % w
    col = lax.broadcasted_iota(jnp.int32, (2 * w, 2 * w), 1)
    band = (col >= qi) & (col <= qi + w)
    lane_q = lax.broadcasted_iota(jnp.int32, (w, LANES), 1)
    low_q = lane_q < HEAD_DIM
    lane_l = lax.broadcasted_iota(jnp.int32, (w, LANES), 1)

    def sub_tile(s, carry):
        r0 = pl.multiple_of(s * w, w)
        min_col = jnp.where(first_tile & (s == 0), w, 0)
        valid = band & (col >= min_col)
        lse_tile = jnp.zeros((w, LANES), F32)
        for p in range(GROUP_WIDTH // LANES):
            lanes = slice(p * LANES, (p + 1) * LANES)
            qp = q_ref[pl.ds(r0, w), lanes]
            zero = jnp.zeros_like(qp)
            qs = jnp.concatenate([jnp.where(low_q, qp, zero), jnp.where(low_q, zero, qp)], axis=0)
            kk = kcat[pl.ds(r0, 2 * w), lanes]
            vv = vcat[pl.ds(r0, 2 * w), lanes]
            sc = lax.dot_general(qs, kk, (((1,), (1,)), ((), ())), preferred_element_type=F32)
            sc = jnp.where(valid, sc, -jnp.inf)
            m = jnp.max(sc, axis=1, keepdims=True)
            pr = jnp.exp(sc - m)
            den = jnp.sum(pr, axis=1, keepdims=True)
            pv = jnp.dot(pr.astype(BF16), vv, preferred_element_type=F32) / den
            o_ref[pl.ds(r0, w), lanes] = jnp.where(low_q, pv[:w], pv[w:]).astype(BF16)
            lse = m + jnp.log(den)
            lse_tile = jnp.where(lane_l == 2 * p, lse[:w], lse_tile)
            lse_tile = jnp.where(lane_l == 2 * p + 1, lse[w:], lse_tile)
        lse_ref[pl.ds(r0, w), :] = lse_tile
        return carry

    lax.fori_loop(0, tq // w, sub_tile, 0)


def _attention_group(q, k, v, group):
    batch, d, length, _ = q.shape
    tq = 512
    w = WINDOW_STEPS
    sub = tq // w
    cur = pl.BlockSpec((None, None, tq, GROUP_WIDTH), lambda b, r, i: (b, r, i, 0))
    prev = pl.BlockSpec((None, None, w, GROUP_WIDTH),
                        lambda b, r, i: (b, r, jnp.maximum(i * sub - 1, 0), 0))
    return pl.pallas_call(
        functools.partial(_attn_kernel, tq=tq),
        grid=(batch, d, length // tq),
        in_specs=[cur, prev, cur, prev, cur],
        out_specs=[
            pl.BlockSpec((None, None, tq, GROUP_WIDTH), lambda b, r, i: (b, r, i, 0)),
            pl.BlockSpec((None, None, tq, LANES), lambda b, r, i: (b, r, i, 0)),
        ],
        out_shape=[
            jax.ShapeDtypeStruct((batch, d, length, GROUP_WIDTH), BF16),
            jax.ShapeDtypeStruct((batch, d, length, LANES), F32),
        ],
        scratch_shapes=[pltpu.VMEM((tq + w, GROUP_WIDTH), BF16), pltpu.VMEM((tq + w, GROUP_WIDTH), BF16)],
        compiler_params=_params(("arbitrary", "arbitrary", "arbitrary")),
        name=f"attn_d{d}",
    )(q, k, k, v, v)


def _gelu_tanh(x):
    return 0.5 * x * (1.0 + jnp.tanh(0.7978845608028654 * (x + 0.044715 * (x * x * x))))


def _lru_kernel(u_ref, g_ref, cw_ref, cb_ref, wa_ref, wx_ref, ba_ref, bx_ref, lam_ref, out_ref,
                ext, a_s, b_s, carry, *, ts):
    cw_tile = LRU_WIDTH // 2
    sub = 8

    @pl.when(pl.program_id(2) == 0)
    def _():
        ext[0:sub, :] = jnp.zeros((sub, cw_tile), F32)
        carry[...] = jnp.zeros((sub, cw_tile), F32)

    u = u_ref[...].astype(F32)
    ext[sub:, :] = u
    conv = cb_ref[...] + cw_ref[0:1, :] * ext[sub - 3:sub - 3 + ts, :]
    for j in range(1, CONV_WIDTH):
        conv = conv + cw_ref[j:j + 1, :] * ext[sub - 3 + j:sub - 3 + j + ts, :]
    ext[0:sub, :] = u[ts - sub:, :]

    xb = conv.astype(BF16)
    blk = 4 * LRU_BLOCK
    ra = jnp.concatenate(
        [jnp.dot(xb[:, q * blk:(q + 1) * blk], wa_ref[q], preferred_element_type=F32)
         for q in range(cw_tile // blk)], axis=1)
    rx = jnp.concatenate(
        [jnp.dot(xb[:, q * blk:(q + 1) * blk], wx_ref[q], preferred_element_type=F32)
         for q in range(cw_tile // blk)], axis=1)
    r = jax.nn.sigmoid(ra + ba_ref[...])
    gate_i = jax.nn.sigmoid(rx + bx_ref[...])
    z = -lam_ref[...]
    softplus = jnp.maximum(z, 0.0) + jnp.log1p(jnp.exp(-jnp.abs(z)))
    log_a = (-LRU_C) * r * softplus
    a = jnp.exp(log_a)
    a_s[...] = a
    b_s[...] = jnp.sqrt(1.0 - a * a) * (gate_i * conv)

    row = lax.broadcasted_iota(jnp.int32, (sub, cw_tile), 0)

    def step(k, h_prev):
        r0 = pl.multiple_of(k * sub, sub)
        a8 = a_s[pl.ds(r0, sub), :]
        b8 = b_s[pl.ds(r0, sub), :]
        for sh in (1, 2, 4):
            keep = row >= sh
            a_sh = jnp.where(keep, pltpu.roll(a8, sh, 0), 1.0)
            b_sh = jnp.where(keep, pltpu.roll(b8, sh, 0), 0.0)
            b8 = a8 * b_sh + b8
            a8 = a8 * a_sh
        h8 = b8 + a8 * h_prev
        b_s[pl.ds(r0, sub), :] = h8
        return jnp.broadcast_to(h8[sub - 1:sub, :], (sub, cw_tile))

    carry[...] = lax.fori_loop(0, ts // sub, step, carry[...])
    out_ref[...] = (_gelu_tanh(g_ref[...].astype(F32)) * b_s[...]).astype(BF16)


def _lru_branch(ug, conv_w, conv_b, wa_bd, wx_bd, ba, bx, lam, batch, seq):
    ts = 512
    cw_tile = LRU_WIDTH // 2
    n_c = LRU_WIDTH // cw_tile
    n_t = seq // ts
    nb = cw_tile // (4 * LRU_BLOCK)
    vec = lambda: pl.BlockSpec((1, cw_tile), lambda b, c, t: (0, c))
    return pl.pallas_call(
        functools.partial(_lru_kernel, ts=ts),
        grid=(batch, n_c, n_t),
        in_specs=[
            pl.BlockSpec((ts, cw_tile), lambda b, c, t: (b * n_t + t, c)),
            pl.BlockSpec((ts, cw_tile), lambda b, c, t: (b * n_t + t, n_c + c)),
            pl.BlockSpec((CONV_WIDTH, cw_tile), lambda b, c, t: (0, c)),
            vec(),
            pl.BlockSpec((nb, 4 * LRU_BLOCK, 4 * LRU_BLOCK), lambda b, c, t: (c, 0, 0)),
            pl.BlockSpec((nb, 4 * LRU_BLOCK, 4 * LRU_BLOCK), lambda b, c, t: (c, 0, 0)),
            vec(), vec(), vec(),
        ],
        out_specs=pl.BlockSpec((ts, cw_tile), lambda b, c, t: (b * n_t + t, c)),
        out_shape=jax.ShapeDtypeStruct((batch * seq, LRU_WIDTH), BF16),
        scratch_shapes=[
            pltpu.VMEM((ts + 8, cw_tile), F32),
            pltpu.VMEM((ts, cw_tile), F32),
            pltpu.VMEM((ts, cw_tile), F32),
            pltpu.VMEM((8, cw_tile), F32),
        ],
        compiler_params=_params(("arbitrary", "arbitrary", "arbitrary")),
        name="lru",
    )(ug, ug, conv_w, conv_b, wa_bd, wx_bd, ba, bx, lam)


def _split_bf16(x):
    hi = x.astype(BF16)
    lo = (x - hi.astype(F32)).astype(BF16)
    return hi, lo


def _mix_kernel(x_ref, o1_ref, o2_ref, o3_ref, l1_ref, l2_ref, l3_ref, lru_ref, gl_ref, bm_ref,
                wpa_ref, wpl_ref, wo_ref, n2_ref, expand_ref, rw_hi_ref, rw_lo_ref, rb_ref, tri_ref,
                xo_ref, h2c_ref, meta_ref, cnt_ref, o2_s, o3_s, l2_s, l3_s, cnt_s):
    tm = x_ref.shape[0]

    @pl.when(pl.program_id(0) == 0)
    def _():
        cnt_s[...] = jnp.zeros(cnt_s.shape, F32)

    for src_o, src_l, dst_o, dst_l in ((o2_ref, l2_ref, o2_s, l2_s), (o3_ref, l3_ref, o3_s, l3_s)):
        d = src_o.shape[0]
        for r in range(d):
            o_r = src_o[r].astype(F32)
            for j in range(GROUP_WIDTH // LANES):
                dst_o[j, pl.ds(r, tm // d, stride=d), :] = o_r[:, j * LANES:(j + 1) * LANES]
            dst_l[pl.ds(r, tm // d, stride=d), :] = src_l[r]
    tok = lambda s: jnp.concatenate([s[j] for j in range(GROUP_WIDTH // LANES)], axis=1)
    l1, l2, l3 = l1_ref[0], l2_s[...], l3_s[...]
    mx = jnp.maximum(jnp.maximum(l1, l2), l3)
    e1, e2, e3 = jnp.exp(l1 - mx), jnp.exp(l2 - mx), jnp.exp(l3 - mx)
    inv = 1.0 / (e1 + e2 + e3)
    attn = None
    for e, o_tok in ((e1, o1_ref[0].astype(F32)), (e2, tok(o2_s)), (e3, tok(o3_s))):
        hi, lo = _split_bf16(e * inv)
        wide = (jnp.dot(hi, expand_ref[...], preferred_element_type=F32)
                + jnp.dot(lo, expand_ref[...], preferred_element_type=F32))
        term = wide * o_tok
        attn = term if attn is None else attn + term
    pa = jnp.dot(attn.astype(BF16), wpa_ref[...], preferred_element_type=F32)
    pr = jnp.dot(lru_ref[...], wpl_ref[...], preferred_element_type=F32)
    gates = jax.nn.sigmoid(gl_ref[...].astype(F32) + bm_ref[...])
    mixed = gates[:, :D_MODEL] * pa + gates[:, D_MODEL:] * pr
    x = x_ref[...] + jnp.dot(mixed.astype(BF16), wo_ref[...], preferred_element_type=F32)
    xo_ref[...] = x

    ms = jnp.mean(x * x, axis=-1, keepdims=True)
    h2 = x * lax.rsqrt(ms + EPS) * n2_ref[...]
    h_hi, h_lo = _split_bf16(h2)
    logits = (jnp.dot(h_hi, rw_hi_ref[...], preferred_element_type=F32)
              + jnp.dot(h_hi, rw_lo_ref[...], preferred_element_type=F32)
              + jnp.dot(h_lo, rw_hi_ref[...], preferred_element_type=F32)) + rb_ref[...]
    comb, g_idx = _route(logits)
    h2c_ref[:, :D_MODEL] = h2
    h2c_ref[:, D_MODEL:] = comb

    lane = lax.broadcasted_iota(jnp.int32, (tm, LANES), 1)
    onehot = lane == g_idx
    before = jnp.dot(tri_ref[...], onehot.astype(BF16), preferred_element_type=F32) + cnt_s[0:1, :]
    rank = jnp.sum(jnp.where(onehot, before, 0.0), axis=1, keepdims=True)
    meta_ref[...] = jnp.where(lane == 0, g_idx.astype(F32), jnp.where(lane == 1, rank, 0.0))
    cnt_s[...] = cnt_s[...] + jnp.sum(onehot.astype(F32), axis=0, keepdims=True)
    cnt_ref[...] = cnt_s[...]


def _route(logits):
    tm = logits.shape[0]
    lane = lax.broadcasted_iota(jnp.int32, (tm, LANES), 1)
    neg = -jnp.inf
    big = jnp.int32(LANES)
    is_g = (lane >= N_EXPERTS) & (lane < N_EXPERTS + N_GROUPS)
    gl = jnp.where(is_g, logits, neg)
    gmax = jnp.max(gl, axis=1, keepdims=True)
    g_idx = jnp.min(jnp.where(gl == gmax, lane, big), axis=1, keepdims=True) - N_EXPERTS
    g_w = 1.0 / jnp.sum(jnp.exp(gl - gmax), axis=1, keepdims=True)
    lo = g_idx * EXPERTS_PER_GROUP
    in_grp = (lane >= lo) & (lane < lo + EXPERTS_PER_GROUP)
    el = jnp.where(in_grp, logits, neg)
    v1 = jnp.max(el, axis=1, keepdims=True)
    i1 = jnp.min(jnp.where(el == v1, lane, big), axis=1, keepdims=True)
    el2 = jnp.where(lane == i1, neg, el)
    v2 = jnp.max(el2, axis=1, keepdims=True)
    i2 = jnp.min(jnp.where(el2 == v2, lane, big), axis=1, keepdims=True)
    e2 = jnp.exp(v2 - v1)
    w1 = g_w / (1.0 + e2)
    w2 = g_w * e2 / (1.0 + e2)
    return jnp.where(lane == i1, w1, 0.0) + jnp.where(lane == i2, w2, 0.0), g_idx


def _mix(x2, o_list, lse_list, lru_out, gl, b_merge, wpa, wpl, wo, norm2, expand, rw_hi, rw_lo, rb, seq):
    t = x2.shape[0]
    tm = IN_PROJ_TILE
    n_seq_tiles = seq // tm
    row = lambda width: pl.BlockSpec((tm, width), lambda i: (i, 0))
    full = lambda a: pl.BlockSpec(a.shape, lambda i: (0,) * a.ndim)
    resid = lambda d, width: pl.BlockSpec((None, d, tm // d, width),
                                          lambda i: (i // n_seq_tiles, 0, i % n_seq_tiles, 0))
    tri = (jnp.arange(tm)[:, None] > jnp.arange(tm)[None, :]).astype(BF16)
    consts = [b_merge, wpa, wpl, wo, norm2, expand, rw_hi, rw_lo, rb, tri]
    return pl.pallas_call(
        _mix_kernel,
        grid=(t // tm,),
        in_specs=[row(D_MODEL)] + [resid(d, GROUP_WIDTH) for d in DILATIONS]
                 + [resid(d, LANES) for d in DILATIONS]
                 + [row(LRU_WIDTH), row(2 * D_MODEL)] + [full(a) for a in consts],
        out_specs=[row(D_MODEL), row(MOE_ROW), row(LANES), pl.BlockSpec((8, LANES), lambda i: (0, 0))],
        out_shape=[
            jax.ShapeDtypeStruct((t, D_MODEL), F32),
            jax.ShapeDtypeStruct((t, MOE_ROW), F32),
            jax.ShapeDtypeStruct((t, LANES), F32),
            jax.ShapeDtypeStruct((8, LANES), F32),
        ],
        scratch_shapes=[pltpu.VMEM((GROUP_WIDTH // LANES, tm, LANES), F32),
                        pltpu.VMEM((GROUP_WIDTH // LANES, tm, LANES), F32),
                        pltpu.VMEM((tm, LANES), F32), pltpu.VMEM((tm, LANES), F32),
                        pltpu.VMEM((8, LANES), F32)],
        compiler_params=_params(("arbitrary",)),
        name="mix",
    )(x2, *o_list, *lse_list, lru_out, gl, *consts)


def _start_row_copies(n_rows, make_copy):
    def issue(r, c):
        make_copy(r).start()
        return c

    lax.fori_loop(0, n_rows, issue, 0, unroll=8)


def _dispatch_kernel(pos_ref, h_ref, xs_init_ref, xs_ref, sem):
    del xs_init_ref
    tm = h_ref.shape[0]
    _start_row_copies(tm, lambda r: pltpu.make_async_copy(
        h_ref.at[pl.ds(r, 1), :], xs_ref.at[pl.ds(pos_ref[0, r], 1), :], sem))
    pltpu.make_async_copy(h_ref, xs_ref.at[pl.ds(0, tm), :], sem).wait()


def _dispatch(h2c, pos, n_rows):
    t = h2c.shape[0]
    tm = MOE_TOKEN_TILE
    return pl.pallas_call(
        _dispatch_kernel,
        grid=(t // tm,),
        in_specs=[
            pl.BlockSpec((1, tm), lambda i: (0, i), memory_space=pltpu.SMEM),
            pl.BlockSpec((tm, MOE_ROW), lambda i: (i, 0)),
            pl.BlockSpec(memory_space=pl.ANY),
        ],
        out_specs=pl.BlockSpec(memory_space=pl.ANY),
        out_shape=jax.ShapeDtypeStruct((n_rows, MOE_ROW), F32),
        scratch_shapes=[pltpu.SemaphoreType.DMA(())],
        input_output_aliases={2: 0},
        compiler_params=_params(("arbitrary",)),
        name="dispatch",
    )(pos, h2c, jnp.zeros((n_rows, MOE_ROW), F32))


def _expert_kernel(tile_group_ref, n_used_ref, xs_ref, wg_ref, wu_ref, wd_ref, y_ref):
    j = pl.program_id(0)

    @pl.when(j < n_used_ref[0])
    def _():
        h = xs_ref[:, :D_MODEL].astype(BF16)
        comb = xs_ref[:, D_MODEL:]
        lane = lax.broadcasted_iota(jnp.int32, comb.shape, 1)
        first = tile_group_ref[j] * EXPERTS_PER_GROUP
        y = None
        for e in range(EXPERTS_PER_GROUP):
            cols = slice(e * D_EXPERT, (e + 1) * D_EXPERT)
            gate = jnp.dot(h, wg_ref[:, cols], preferred_element_type=F32)
            up = jnp.dot(h, wu_ref[:, cols], preferred_element_type=F32)
            w_e = jnp.sum(jnp.where(lane == first + e, comb, 0.0), axis=1, keepdims=True)
            act = (gate * jax.nn.sigmoid(gate) * up * w_e).astype(BF16)
            part = jnp.dot(act, wd_ref[cols, :], preferred_element_type=F32)
            y = part if y is None else y + part
        y_ref[...] = y

    @pl.when(j >= n_used_ref[0])
    def _():
        y_ref[...] = jnp.zeros(y_ref.shape, F32)


def _experts(xs, tile_group, n_used, wg, wu, wd):
    n_rows = xs.shape[0]
    tg = MOE_GROUP_TILE
    group_w = EXPERTS_PER_GROUP * D_EXPERT
    return pl.pallas_call(
        _expert_kernel,
        grid_spec=pltpu.PrefetchScalarGridSpec(
            num_scalar_prefetch=2,
            grid=(n_rows // tg,),
            in_specs=[
                pl.BlockSpec((tg, MOE_ROW), lambda j, tgr, nu: (j, 0)),
                pl.BlockSpec((None, D_MODEL, group_w), lambda j, tgr, nu: (tgr[j], 0, 0)),
                pl.BlockSpec((None, D_MODEL, group_w), lambda j, tgr, nu: (tgr[j], 0, 0)),
                pl.BlockSpec((None, group_w, D_MODEL), lambda j, tgr, nu: (tgr[j], 0, 0)),
            ],
            out_specs=pl.BlockSpec((tg, D_MODEL), lambda j, tgr, nu: (j, 0)),
        ),
        out_shape=jax.ShapeDtypeStruct((n_rows, D_MODEL), F32),
        compiler_params=_params(("arbitrary",)),
        name="experts",
    )(tile_group, n_used, xs, wg, wu, wd)


def _combine_kernel(pos_ref, x_ref, y_ref, fn_ref, out_ref, ybuf, sem, *, final_norm):
    tm = ybuf.shape[0]
    _start_row_copies(tm, lambda r: pltpu.make_async_copy(
        y_ref.at[pl.ds(pos_ref[0, r], 1), :], ybuf.at[pl.ds(r, 1), :], sem))
    pltpu.make_async_copy(y_ref.at[pl.ds(0, tm), :], ybuf, sem).wait()
    x = x_ref[...] + ybuf[...]
    if final_norm:
        ms = jnp.mean(x * x, axis=-1, keepdims=True)
        x = x * lax.rsqrt(ms + EPS) * fn_ref[...]
    out_ref[...] = x


def _combine(x_mid, y, pos, final_scale, final_norm):
    t = x_mid.shape[0]
    tm = MOE_TOKEN_TILE
    return pl.pallas_call(
        functools.partial(_combine_kernel, final_norm=final_norm),
        grid=(t // tm,),
        in_specs=[
            pl.BlockSpec((1, tm), lambda i: (0, i), memory_space=pltpu.SMEM),
            pl.BlockSpec((tm, D_MODEL), lambda i: (i, 0)),
            pl.BlockSpec(memory_space=pl.ANY),
            pl.BlockSpec((1, D_MODEL), lambda i: (0, 0)),
        ],
        out_specs=pl.BlockSpec((tm, D_MODEL), lambda i: (i, 0)),
        out_shape=jax.ShapeDtypeStruct((t, D_MODEL), F32),
        scratch_shapes=[pltpu.VMEM((tm, D_MODEL), F32), pltpu.SemaphoreType.DMA(())],
        compiler_params=_params(("arbitrary",)),
        name="combine",
    )(pos, x_mid, y, final_scale)


def _moe(x_mid, h2c, meta, counts, wg, wu, wd, final_scale, final_norm):
    t = x_mid.shape[0]
    tg = MOE_GROUP_TILE
    n_rows = t + N_GROUPS * tg
    cnt = counts[0, :N_GROUPS].astype(jnp.int32)
    padded = (cnt + tg - 1) // tg * tg
    ends = jnp.cumsum(padded)
    starts = ends - padded
    pos = (starts[meta[:, 0].astype(jnp.int32)] + meta[:, 1].astype(jnp.int32)).reshape(1, t)
    tile_start = jnp.arange(n_rows // tg, dtype=jnp.int32) * tg
    tile_group = jnp.minimum(jnp.sum(tile_start[:, None] >= ends[None, :], axis=1), N_GROUPS - 1)
    n_used = (ends[-1] // tg).reshape(1)
    xs = _dispatch(h2c, pos, n_rows)
    y = _experts(xs, tile_group.astype(jnp.int32), n_used.astype(jnp.int32), wg, wu, wd)
    return _combine(x_mid, y, pos, final_scale, final_norm)


def _rotary_tables(seq):
    inv = ROPE_THETA ** (-jnp.arange(0, HEAD_DIM, 2, dtype=F32) / HEAD_DIM)
    ang = jnp.arange(seq, dtype=F32)[:, None] * inv[None, :]
    cos, sin = jnp.cos(ang), jnp.sin(ang)
    cos_t, sin_t = jnp.tile(cos, (1, 4)), jnp.concatenate([-sin, sin, -sin, sin], axis=1)

    def residue_major(tab):
        out = []
        for d in DILATIONS:
            tiles = tab.reshape(seq // IN_PROJ_TILE, IN_PROJ_TILE // d, d, LANES)
            out.append(tiles.transpose(0, 2, 1, 3).reshape(seq, LANES))
        return jnp.stack(out)

    return residue_major(cos_t), residue_major(sin_t)


def _block_diag(w):
    nb = w.shape[0] // 4
    w4 = w.reshape(nb, 4, LRU_BLOCK, LRU_BLOCK)
    eye = jnp.eye(4, dtype=w.dtype)
    return jnp.einsum('qaij,ab->qaibj', w4, eye).reshape(nb, 4 * LRU_BLOCK, 4 * LRU_BLOCK)


def kernel(x, norm1_scale, w_in, b_merge, conv_w, conv_b, lru_wa, lru_ba, lru_wx, lru_bx, lru_lambda,
           w_proj_attn, w_proj_lru, w_out, norm2_scale, router_group_w, router_group_b, router_expert_w,
           router_expert_b, expert_w_gate, expert_w_up, expert_w_down, final_norm_scale):
    batch, seq, _ = x.shape
    depth = w_in.shape[0]
    cos_t, sin_t = _rotary_tables(seq)
    expand = (jnp.arange(LANES)[:, None] == (jnp.arange(GROUP_WIDTH)[None, :] // HEAD_DIM)).astype(BF16)
    row = lambda v: v.reshape(1, -1)
    x2 = x.reshape(batch * seq, D_MODEL)
    for l in range(depth):
        *qkv, ug, gl = _in_proj(x2, row(norm1_scale[l]), w_in[l].astype(BF16), cos_t, sin_t, batch, seq)
        o_list, lse_list = [], []
        for g in range(len(DILATIONS)):
            o, lse = _attention_group(qkv[g], qkv[3 + g], qkv[6 + g], g)
            o_list.append(o)
            lse_list.append(lse)
        lru_out = _lru_branch(ug, conv_w[l], row(conv_b[l]), _block_diag(lru_wa[l]).astype(BF16),
                              _block_diag(lru_wx[l]).astype(BF16), row(lru_ba[l]), row(lru_bx[l]),
                              row(lru_lambda[l]), batch, seq)
        rw = jnp.zeros((D_MODEL, LANES), F32)
        rw = rw.at[:, :N_EXPERTS].set(router_expert_w[l]).at[:, N_EXPERTS:N_EXPERTS + N_GROUPS].set(
            router_group_w[l])
        rb = jnp.zeros((1, LANES), F32)
        rb = rb.at[0, :N_EXPERTS].set(router_expert_b[l]).at[0, N_EXPERTS:N_EXPERTS + N_GROUPS].set(
            router_group_b[l])
        rw_hi = rw.astype(BF16)
        rw_lo = (rw - rw_hi.astype(F32)).astype(BF16)
        x_mid, h2c, meta, counts = _mix(x2, o_list, lse_list, lru_out, gl, row(b_merge[l]),
                                        w_proj_attn[l].astype(BF16), w_proj_lru[l].astype(BF16),
                                        w_out[l].astype(BF16), row(norm2_scale[l]), expand, rw_hi, rw_lo, rb,
                                        seq)
        group_cols = lambda w: w.astype(BF16).reshape(N_GROUPS, EXPERTS_PER_GROUP, D_MODEL, D_EXPERT).transpose(
            0, 2, 1, 3).reshape(N_GROUPS, D_MODEL, EXPERTS_PER_GROUP * D_EXPERT)
        wd = expert_w_down[l].astype(BF16).reshape(N_GROUPS, EXPERTS_PER_GROUP * D_EXPERT, D_MODEL)
        x2 = _moe(x_mid, h2c, meta, counts, group_cols(expert_w_gate[l]), group_cols(expert_w_up[l]), wd,
                  row(final_norm_scale), final_norm=(l == depth - 1))
    return x2.reshape(batch, seq, D_MODEL)
```

```python
import functools

import jax
import jax.numpy as jnp
from jax import lax
from jax.experimental import pallas as pl
from jax.experimental.pallas import tpu as pltpu

D_MODEL = 1024
HEAD_DIM = 64
GROUP_WIDTH = 512
DILATIONS = (1, 4, 16)
WINDOW_STEPS = 128
ATTN_WIDTH = 3 * GROUP_WIDTH
LRU_WIDTH = 1024
LRU_BLOCK = 64
N_IN = 3 * ATTN_WIDTH + 2 * LRU_WIDTH + 2 * D_MODEL
CONV_WIDTH = 4
LRU_C = 8.0
N_GROUPS = 4
EXPERTS_PER_GROUP = 4
N_EXPERTS = 16
D_EXPERT = 512
EPS = 1e-6
ROPE_THETA = 10000.0
LANES = 128
VMEM_LIMIT = 56 * 1024 * 1024
IN_PROJ_TILE = 512
ATTN_QUERY_BLOCK = 128
MOE_ROW = D_MODEL + LANES
MOE_TOKEN_TILE = 512
MOE_GROUP_TILE = 512

F32 = jnp.float32
BF16 = jnp.bfloat16


def _sigmoid(x):
    return 0.5 * jnp.tanh(0.5 * x) + 0.5


def _params(semantics):
    return pltpu.CompilerParams(dimension_semantics=semantics, vmem_limit_bytes=VMEM_LIMIT)


def _in_proj_kernel(x_ref, scale_ref, w_ref, cos_ref, sin_ref, *rest):
    qkv_refs, (ug_ref, gl_ref, h32_ref, hn_ref) = rest[:9], rest[9:]
    x = x_ref[...]
    tm = x.shape[0]
    ms = jnp.mean(x * x, axis=-1, keepdims=True)
    hn = x * lax.rsqrt(ms + EPS) * scale_ref[...]
    hn_ref[0] = hn.astype(BF16)
    n_lane_blocks = D_MODEL // LANES
    for j in range(n_lane_blocks):
        h32_ref[j] = hn[:, j * LANES:(j + 1) * LANES]
    for g in (1, 2):
        d = DILATIONS[g]
        rows = tm // d
        for r in range(d):
            for j in range(n_lane_blocks):
                hn_ref[g, r * rows:(r + 1) * rows, j * LANES:(j + 1) * LANES] = (
                    h32_ref[j, pl.ds(r, rows, stride=d), :].astype(BF16))
    lane = lax.broadcasted_iota(jnp.int32, (tm, LANES), 1)
    first_half = (lane % HEAD_DIM) < (HEAD_DIM // 2)
    n_chunks = N_IN // GROUP_WIDTH
    for c in range(n_chunks):
        g = c % 3 if c < 9 else 0
        res = jnp.dot(hn_ref[g], w_ref[:, c * GROUP_WIDTH:(c + 1) * GROUP_WIDTH],
                      preferred_element_type=F32)
        if c < 6:
            d = DILATIONS[g]
            cos, sin = (jnp.concatenate([t_ref[pl.ds(r, tm // d, stride=d), :] for r in range(d)], axis=0)
                        if d > 1 else t_ref[...] for t_ref in (cos_ref, sin_ref))
            parts = []
            for s in range(GROUP_WIDTH // LANES):
                blk = res[:, s * LANES:(s + 1) * LANES]
                partner = jnp.where(first_half, pltpu.roll(blk, LANES - HEAD_DIM // 2, 1),
                                    pltpu.roll(blk, HEAD_DIM // 2, 1))
                o = blk * cos + partner * sin
                if c < 3:
                    o = o * (HEAD_DIM ** -0.5)
                parts.append(o)
            res = jnp.concatenate(parts, axis=1)
        if c < 9:
            d = DILATIONS[g]
            qkv_refs[c][...] = res.reshape(d, tm // d, GROUP_WIDTH).astype(BF16)
        elif c < 13:
            ug_ref[:, (c - 9) * GROUP_WIDTH:(c - 8) * GROUP_WIDTH] = res.astype(BF16)
        else:
            gl_ref[:, (c - 13) * GROUP_WIDTH:(c - 12) * GROUP_WIDTH] = res.astype(BF16)


def _in_proj(x2, scale, w_bf, cos_t, sin_t, batch, seq):
    t = x2.shape[0]
    tm = IN_PROJ_TILE
    n_seq_tiles = seq // tm
    qkv_specs, qkv_shapes = [], []
    for c in range(9):
        d = DILATIONS[c % 3]
        qkv_specs.append(pl.BlockSpec((None, d, tm // d, GROUP_WIDTH),
                                      lambda i: (i // n_seq_tiles, 0, i % n_seq_tiles, 0)))
        qkv_shapes.append(jax.ShapeDtypeStruct((batch, d, seq // d, GROUP_WIDTH), BF16))
    return pl.pallas_call(
        _in_proj_kernel,
        grid=(t // tm,),
        in_specs=[
            pl.BlockSpec((tm, D_MODEL), lambda i: (i, 0)),
            pl.BlockSpec((1, D_MODEL), lambda i: (0, 0)),
            pl.BlockSpec((D_MODEL, N_IN), lambda i: (0, 0), pipeline_mode=pl.Buffered(1)),
            pl.BlockSpec((tm, LANES), lambda i: (i % n_seq_tiles, 0)),
            pl.BlockSpec((tm, LANES), lambda i: (i % n_seq_tiles, 0)),
        ],
        out_specs=qkv_specs + [
            pl.BlockSpec((tm, 2 * LRU_WIDTH), lambda i: (i, 0)),
            pl.BlockSpec((tm, 2 * D_MODEL), lambda i: (i, 0)),
        ],
        out_shape=qkv_shapes + [
            jax.ShapeDtypeStruct((t, 2 * LRU_WIDTH), BF16),
            jax.ShapeDtypeStruct((t, 2 * D_MODEL), BF16),
        ],
        scratch_shapes=[pltpu.VMEM((D_MODEL // LANES, tm, LANES), F32), pltpu.VMEM((3, tm, D_MODEL), BF16)],
        compiler_params=_params(("arbitrary",)),
        name="in_proj",
    )(x2, scale, w_bf, cos_t, sin_t)


def _attn_kernel(q_ref, kp_ref, kc_ref, vp_ref, vc_ref, o_ref, lse_ref, kcat, vcat, *, tq):
    w = WINDOW_STEPS
    qb = ATTN_QUERY_BLOCK
    kcat[0:w, :] = kp_ref[...]
    kcat[w:w + tq, :] = kc_ref[...]
    kcat[w + tq:, :] = jnp.zeros((qb, GROUP_WIDTH), BF16)
    vcat[0:w, :] = vp_ref[...]
    vcat[w:w + tq, :] = vc_ref[...]
    vcat[w + tq:, :] = jnp.zeros((qb, GROUP_WIDTH), BF16)
    first_tile = pl.program_id(2) == 0

    qi = lax.broadcasted_iota(jnp.int32, (2 * qb, 2 * w), 0) % qb
    col = lax.broadcasted_iota(jnp.int32, (2 * qb, 2 * w), 1)
    band = (col >= qi) & (col <= qi + w)
    lane_q = lax.broadcasted_iota(jnp.int32, (qb, LANES), 1)
    low_q = lane_q < HEAD_DIM

    n_pairs = GROUP_WIDTH // LANES

    def query_block(s, carry):
        r0 = pl.multiple_of(s * qb, qb)
        min_col = jnp.where(first_tile, w - r0, 0)
        valid = band & (col >= min_col)
        lse_tile = jnp.zeros((qb, LANES), F32)
        for p in range(n_pairs):
            lanes = slice(p * LANES, (p + 1) * LANES)
            qp = q_ref[pl.ds(r0, qb), lanes]
            zero = jnp.zeros_like(qp)
            qs = jnp.concatenate([jnp.where(low_q, qp, zero), jnp.where(low_q, zero, qp)], axis=0)
            kk = kcat[pl.ds(r0, 2 * w), lanes]
            vv = vcat[pl.ds(r0, 2 * w), lanes]
            sc = lax.dot_general(qs, kk, (((1,), (1,)), ((), ())), preferred_element_type=F32)
            sc = jnp.where(valid, sc, -jnp.inf)
            m = jnp.max(sc, axis=1, keepdims=True)
            pr = jnp.exp(sc - m)
            den = jnp.sum(pr, axis=1, keepdims=True)
            pv = jnp.dot(pr.astype(BF16), vv, preferred_element_type=F32) / den
            o_ref[pl.ds(r0, qb), lanes] = jnp.where(low_q, pv[:qb], pv[qb:]).astype(BF16)
            lse = m + jnp.log(den)
            lse_tile = jnp.where(lane_q == 2 * p, lse[:qb], lse_tile)
            lse_tile = jnp.where(lane_q == 2 * p + 1, lse[qb:], lse_tile)
        lse_ref[pl.ds(r0, qb), :] = lse_tile
        return carry

    lax.fori_loop(0, tq // qb, query_block, 0)


def _attention_group(q, k, v, group):
    batch, d, length, _ = q.shape
    tq = 512
    w = WINDOW_STEPS
    sub = tq // w
    cur = pl.BlockSpec((None, None, tq, GROUP_WIDTH), lambda b, r, i: (b, r, i, 0))
    prev = pl.BlockSpec((None, None, w, GROUP_WIDTH),
                        lambda b, r, i: (b, r, jnp.maximum(i * sub - 1, 0), 0))
    return pl.pallas_call(
        functools.partial(_attn_kernel, tq=tq),
        grid=(batch, d, length // tq),
        in_specs=[cur, prev, cur, prev, cur],
        out_specs=[
            pl.BlockSpec((None, None, tq, GROUP_WIDTH), lambda b, r, i: (b, r, i, 0)),
            pl.BlockSpec((None, None, tq, LANES), lambda b, r, i: (b, r, i, 0)),
        ],
        out_shape=[
            jax.ShapeDtypeStruct((batch, d, length, GROUP_WIDTH), BF16),
            jax.ShapeDtypeStruct((batch, d, length, LANES), F32),
        ],
        scratch_shapes=[pltpu.VMEM((tq + w + ATTN_QUERY_BLOCK, GROUP_WIDTH), BF16)] * 2,
        compiler_params=_params(("arbitrary", "arbitrary", "arbitrary")),
        name=f"attn_d{d}",
    )(q, k, k, v, v)


def _gelu_tanh(x):
    return 0.5 * x * (1.0 + jnp.tanh(0.7978845608028654 * (x + 0.044715 * (x * x * x))))


def _lru_kernel(u_ref, g_ref, cw_ref, cb_ref, wa_ref, wx_ref, ba_ref, bx_ref, lam_ref, out_ref,
                ext, a_s, b_s, carry, *, ts):
    cw_tile = LRU_WIDTH // 2
    sub = 8

    @pl.when(pl.program_id(2) == 0)
    def _():
        ext[0:sub, :] = jnp.zeros((sub, cw_tile), F32)
        carry[...] = jnp.zeros((sub, cw_tile), F32)

    u = u_ref[...].astype(F32)
    ext[sub:, :] = u
    conv = cb_ref[...] + cw_ref[0:1, :] * ext[sub - 3:sub - 3 + ts, :]
    for j in range(1, CONV_WIDTH):
        conv = conv + cw_ref[j:j + 1, :] * ext[sub - 3 + j:sub - 3 + j + ts, :]
    ext[0:sub, :] = u[ts - sub:, :]

    xb = conv.astype(BF16)
    blk = 4 * LRU_BLOCK
    ra = jnp.concatenate(
        [jnp.dot(xb[:, q * blk:(q + 1) * blk], wa_ref[q], preferred_element_type=F32)
         for q in range(cw_tile // blk)], axis=1)
    rx = jnp.concatenate(
        [jnp.dot(xb[:, q * blk:(q + 1) * blk], wx_ref[q], preferred_element_type=F32)
         for q in range(cw_tile // blk)], axis=1)
    r = _sigmoid(ra + ba_ref[...])
    gate_i = _sigmoid(rx + bx_ref[...])
    z = -lam_ref[...]
    softplus = jnp.maximum(z, 0.0) + jnp.log1p(jnp.exp(-jnp.abs(z)))
    log_a = (-LRU_C) * r * softplus
    a = jnp.exp(log_a)
    a_s[...] = a
    one_m_a2 = 1.0 - a * a
    root = jnp.where(one_m_a2 > 0.0, one_m_a2 * lax.rsqrt(one_m_a2), 0.0)
    b_s[...] = root * (gate_i * conv)

    row = lax.broadcasted_iota(jnp.int32, (sub, cw_tile), 0)

    def step(k, h_prev):
        r0 = pl.multiple_of(k * sub, sub)
        a8 = a_s[pl.ds(r0, sub), :]
        b8 = b_s[pl.ds(r0, sub), :]
        for sh in (1, 2, 4):
            keep = row >= sh
            a_sh = jnp.where(keep, pltpu.roll(a8, sh, 0), 1.0)
            b_sh = jnp.where(keep, pltpu.roll(b8, sh, 0), 0.0)
            b8 = a8 * b_sh + b8
            a8 = a8 * a_sh
        h8 = b8 + a8 * h_prev
        b_s[pl.ds(r0, sub), :] = h8
        return jnp.broadcast_to(h8[sub - 1:sub, :], (sub, cw_tile))

    carry[...] = lax.fori_loop(0, ts // sub, step, carry[...])
    out_ref[...] = (_gelu_tanh(g_ref[...].astype(F32)) * b_s[...]).astype(BF16)


def _lru_branch(ug, conv_w, conv_b, wa_bd, wx_bd, ba, bx, lam, batch, seq):
    ts = 512
    cw_tile = LRU_WIDTH // 2
    n_c = LRU_WIDTH // cw_tile
    n_t = seq // ts
    nb = cw_tile // (4 * LRU_BLOCK)
    vec = lambda: pl.BlockSpec((1, cw_tile), lambda b, c, t: (0, c))
    return pl.pallas_call(
        functools.partial(_lru_kernel, ts=ts),
        grid=(batch, n_c, n_t),
        in_specs=[
            pl.BlockSpec((ts, cw_tile), lambda b, c, t: (b * n_t + t, c)),
            pl.BlockSpec((ts, cw_tile), lambda b, c, t: (b * n_t + t, n_c + c)),
            pl.BlockSpec((CONV_WIDTH, cw_tile), lambda b, c, t: (0, c)),
            vec(),
            pl.BlockSpec((nb, 4 * LRU_BLOCK, 4 * LRU_BLOCK), lambda b, c, t: (c, 0, 0)),
            pl.BlockSpec((nb, 4 * LRU_BLOCK, 4 * LRU_BLOCK), lambda b, c, t: (c, 0, 0)),
            vec(), vec(), vec(),
        ],
        out_specs=pl.BlockSpec((ts, cw_tile), lambda b, c, t: (b * n_t + t, c)),
        out_shape=jax.ShapeDtypeStruct((batch * seq, LRU_WIDTH), BF16),
        scratch_shapes=[
            pltpu.VMEM((ts + 8, cw_tile), F32),
            pltpu.VMEM((ts, cw_tile), F32),
            pltpu.VMEM((ts, cw_tile), F32),
            pltpu.VMEM((8, cw_tile), F32),
        ],
        compiler_params=_params(("arbitrary", "arbitrary", "arbitrary")),
        name="lru",
    )(ug, ug, conv_w, conv_b, wa_bd, wx_bd, ba, bx, lam)


def _split_bf16(x):
    hi = x.astype(BF16)
    lo = (x - hi.astype(F32)).astype(BF16)
    return hi, lo


def _mix_kernel(x_ref, o1_ref, o2_ref, o3_ref, l1_ref, l2_ref, l3_ref, lru_ref, gl_ref, bm_ref,
                wpa_ref, wpl_ref, wo_ref, n2_ref, expand_ref, rw_ref, rb_ref, tri_ref,
                xo_ref, h2c_ref, meta_ref, cnt_ref, o2_s, o3_s, l2_s, l3_s, cnt_s):
    tm = x_ref.shape[0]

    @pl.when(pl.program_id(0) == 0)
    def _():
        cnt_s[...] = jnp.zeros(cnt_s.shape, F32)

    for src_o, src_l, dst_o, dst_l in ((o2_ref, l2_ref, o2_s, l2_s), (o3_ref, l3_ref, o3_s, l3_s)):
        d = src_o.shape[0]
        for r in range(d):
            o_r = src_o[r].astype(F32)
            for j in range(GROUP_WIDTH // LANES):
                dst_o[j, pl.ds(r, tm // d, stride=d), :] = o_r[:, j * LANES:(j + 1) * LANES]
            dst_l[pl.ds(r, tm // d, stride=d), :] = src_l[r]
    tok = lambda s: jnp.concatenate([s[j] for j in range(GROUP_WIDTH // LANES)], axis=1)
    l1, l2, l3 = l1_ref[0], l2_s[...], l3_s[...]
    mx = jnp.maximum(jnp.maximum(l1, l2), l3)
    e1, e2, e3 = jnp.exp(l1 - mx), jnp.exp(l2 - mx), jnp.exp(l3 - mx)
    inv = 1.0 / (e1 + e2 + e3)
    attn = None
    for e, o_tok in ((e1, o1_ref[0].astype(F32)), (e2, tok(o2_s)), (e3, tok(o3_s))):
        hi, lo = _split_bf16(e * inv)
        wide = jnp.dot(jnp.concatenate([hi, lo], axis=1), expand_ref[...], preferred_element_type=F32)
        term = wide * o_tok
        attn = term if attn is None else attn + term
    pa = jnp.dot(attn.astype(BF16), wpa_ref[...], preferred_element_type=F32)
    pr = jnp.dot(lru_ref[...], wpl_ref[...], preferred_element_type=F32)
    gates = _sigmoid(gl_ref[...].astype(F32) + bm_ref[...])
    mixed = gates[:, :D_MODEL] * pa + gates[:, D_MODEL:] * pr
    x = x_ref[...] + jnp.dot(mixed.astype(BF16), wo_ref[...], preferred_element_type=F32)
    xo_ref[...] = x

    ms = jnp.mean(x * x, axis=-1, keepdims=True)
    h2 = x * lax.rsqrt(ms + EPS) * n2_ref[...]
    h_hi, h_lo = _split_bf16(h2)
    both = jnp.dot(h_hi, rw_ref[...], preferred_element_type=F32)
    logits = (both[:, :LANES] + both[:, LANES:]
              + jnp.dot(h_lo, rw_ref[:, :LANES], preferred_element_type=F32)) + rb_ref[...]
    comb, g_idx = _route(logits)
    h2c_ref[:, :D_MODEL] = h2
    h2c_ref[:, D_MODEL:] = comb

    lane = lax.broadcasted_iota(jnp.int32, (tm, LANES), 1)
    onehot = lane == g_idx
    before = jnp.dot(tri_ref[...], onehot.astype(BF16), preferred_element_type=F32) + cnt_s[0:1, :]
    rank = jnp.sum(jnp.where(onehot, before, 0.0), axis=1, keepdims=True)
    meta_ref[...] = jnp.where(lane == 0, g_idx.astype(F32), jnp.where(lane == 1, rank, 0.0))
    cnt_s[...] = cnt_s[...] + jnp.sum(onehot.astype(F32), axis=0, keepdims=True)
    cnt_ref[...] = cnt_s[...]


def _route(logits):
    tm = logits.shape[0]
    lane = lax.broadcasted_iota(jnp.int32, (tm, LANES), 1)
    neg = -jnp.inf
    big = jnp.int32(LANES)
    is_g = (lane >= N_EXPERTS) & (lane < N_EXPERTS + N_GROUPS)
    gl = jnp.where(is_g, logits, neg)
    gmax = jnp.max(gl, axis=1, keepdims=True)
    g_idx = jnp.min(jnp.where(gl == gmax, lane, big), axis=1, keepdims=True) - N_EXPERTS
    g_w = 1.0 / jnp.sum(jnp.exp(gl - gmax), axis=1, keepdims=True)
    lo = g_idx * EXPERTS_PER_GROUP
    in_grp = (lane >= lo) & (lane < lo + EXPERTS_PER_GROUP)
    el = jnp.where(in_grp, logits, neg)
    v1 = jnp.max(el, axis=1, keepdims=True)
    i1 = jnp.min(jnp.where(el == v1, lane, big), axis=1, keepdims=True)
    el2 = jnp.where(lane == i1, neg, el)
    v2 = jnp.max(el2, axis=1, keepdims=True)
    i2 = jnp.min(jnp.where(el2 == v2, lane, big), axis=1, keepdims=True)
    e2 = jnp.exp(v2 - v1)
    w1 = g_w / (1.0 + e2)
    w2 = g_w * e2 / (1.0 + e2)
    return jnp.where(lane == i1, w1, 0.0) + jnp.where(lane == i2, w2, 0.0), g_idx


def _mix(x2, o_list, lse_list, lru_out, gl, b_merge, wpa, wpl, wo, norm2, expand, rw_split, rb, seq):
    t = x2.shape[0]
    tm = IN_PROJ_TILE
    n_seq_tiles = seq // tm
    row = lambda width: pl.BlockSpec((tm, width), lambda i: (i, 0))
    full = lambda a: pl.BlockSpec(a.shape, lambda i: (0,) * a.ndim)
    resid = lambda d, width: pl.BlockSpec((None, d, tm // d, width),
                                          lambda i: (i // n_seq_tiles, 0, i % n_seq_tiles, 0))
    tri = (jnp.arange(tm)[:, None] > jnp.arange(tm)[None, :]).astype(BF16)
    consts = [b_merge, wpa, wpl, wo, norm2, expand, rw_split, rb, tri]
    return pl.pallas_call(
        _mix_kernel,
        grid=(t // tm,),
        in_specs=[row(D_MODEL)] + [resid(d, GROUP_WIDTH) for d in DILATIONS]
                 + [resid(d, LANES) for d in DILATIONS]
                 + [row(LRU_WIDTH), row(2 * D_MODEL)] + [full(a) for a in consts],
        out_specs=[row(D_MODEL), row(MOE_ROW), row(LANES), pl.BlockSpec((8, LANES), lambda i: (0, 0))],
        out_shape=[
            jax.ShapeDtypeStruct((t, D_MODEL), F32),
            jax.ShapeDtypeStruct((t, MOE_ROW), F32),
            jax.ShapeDtypeStruct((t, LANES), F32),
            jax.ShapeDtypeStruct((8, LANES), F32),
        ],
        scratch_shapes=[pltpu.VMEM((GROUP_WIDTH // LANES, tm, LANES), F32),
                        pltpu.VMEM((GROUP_WIDTH // LANES, tm, LANES), F32),
                        pltpu.VMEM((tm, LANES), F32), pltpu.VMEM((tm, LANES), F32),
                        pltpu.VMEM((8, LANES), F32)],
        compiler_params=_params(("arbitrary",)),
        name="mix",
    )(x2, *o_list, *lse_list, lru_out, gl, *consts)


def _start_row_copies(n_rows, make_copy):
    def issue(half, c):
        make_copy(2 * half).start(priority=0)
        make_copy(2 * half + 1).start(priority=1)
        return c

    lax.fori_loop(0, n_rows // 2, issue, 0, unroll=4)


def _dispatch_kernel(pos_ref, h_ref, xs_init_ref, xs_ref, sem):
    del xs_init_ref
    tm = h_ref.shape[0]
    _start_row_copies(tm, lambda r: pltpu.make_async_copy(
        h_ref.at[pl.ds(r, 1), :], xs_ref.at[pl.ds(pos_ref[0, r], 1), :], sem))
    pltpu.make_async_copy(h_ref, xs_ref.at[pl.ds(0, tm), :], sem).wait()


def _dispatch(h2c, pos, n_rows):
    t = h2c.shape[0]
    tm = MOE_TOKEN_TILE
    return pl.pallas_call(
        _dispatch_kernel,
        grid=(t // tm,),
        in_specs=[
            pl.BlockSpec((1, tm), lambda i: (0, i), memory_space=pltpu.SMEM),
            pl.BlockSpec((tm, MOE_ROW), lambda i: (i, 0)),
            pl.BlockSpec(memory_space=pl.ANY),
        ],
        out_specs=pl.BlockSpec(memory_space=pl.ANY),
        out_shape=jax.ShapeDtypeStruct((n_rows, MOE_ROW), F32),
        scratch_shapes=[pltpu.SemaphoreType.DMA(())],
        input_output_aliases={2: 0},
        compiler_params=_params(("arbitrary",)),
        name="dispatch",
    )(pos, h2c, jnp.zeros((n_rows, MOE_ROW), F32))


def _expert_kernel(tile_group_ref, n_used_ref, xs_ref, wg_ref, wu_ref, wd_ref, y_ref):
    j = pl.program_id(0)

    @pl.when(j < n_used_ref[0])
    def _():
        h = xs_ref[:, :D_MODEL].astype(BF16)
        comb = xs_ref[:, D_MODEL:]
        lane = lax.broadcasted_iota(jnp.int32, comb.shape, 1)
        first = tile_group_ref[j] * EXPERTS_PER_GROUP
        y = None
        for e in range(EXPERTS_PER_GROUP):
            gate = jnp.dot(h, wg_ref[e], preferred_element_type=F32)
            up = jnp.dot(h, wu_ref[e], preferred_element_type=F32)
            w_e = jnp.sum(jnp.where(lane == first + e, comb, 0.0), axis=1, keepdims=True)
            act = (gate * _sigmoid(gate) * up * w_e).astype(BF16)
            part = jnp.dot(act, wd_ref[e], preferred_element_type=F32)
            y = part if y is None else y + part
        y_ref[...] = y

    @pl.when(j >= n_used_ref[0])
    def _():
        y_ref[...] = jnp.zeros(y_ref.shape, F32)


def _experts(xs, tile_group, n_used, wg, wu, wd):
    n_rows = xs.shape[0]
    tg = MOE_GROUP_TILE
    epg = EXPERTS_PER_GROUP
    return pl.pallas_call(
        _expert_kernel,
        grid_spec=pltpu.PrefetchScalarGridSpec(
            num_scalar_prefetch=2,
            grid=(n_rows // tg,),
            in_specs=[
                pl.BlockSpec((tg, MOE_ROW), lambda j, tgr, nu: (j, 0)),
                pl.BlockSpec((epg, D_MODEL, D_EXPERT), lambda j, tgr, nu: (tgr[j], 0, 0)),
                pl.BlockSpec((epg, D_MODEL, D_EXPERT), lambda j, tgr, nu: (tgr[j], 0, 0)),
                pl.BlockSpec((epg, D_EXPERT, D_MODEL), lambda j, tgr, nu: (tgr[j], 0, 0)),
            ],
            out_specs=pl.BlockSpec((tg, D_MODEL), lambda j, tgr, nu: (j, 0)),
        ),
        out_shape=jax.ShapeDtypeStruct((n_rows, D_MODEL), F32),
        compiler_params=_params(("arbitrary",)),
        name="experts",
    )(tile_group, n_used, xs, wg, wu, wd)


def _combine_kernel(pos_ref, x_ref, y_ref, fn_ref, out_ref, ybuf, sem, *, final_norm):
    tm = ybuf.shape[0]
    _start_row_copies(tm, lambda r: pltpu.make_async_copy(
        y_ref.at[pl.ds(pos_ref[0, r], 1), :], ybuf.at[pl.ds(r, 1), :], sem))
    pltpu.make_async_copy(y_ref.at[pl.ds(0, tm), :], ybuf, sem).wait()
    x = x_ref[...] + ybuf[...]
    if final_norm:
        ms = jnp.mean(x * x, axis=-1, keepdims=True)
        x = x * lax.rsqrt(ms + EPS) * fn_ref[...]
    out_ref[...] = x


def _combine(x_mid, y, pos, final_scale, final_norm):
    t = x_mid.shape[0]
    tm = MOE_TOKEN_TILE
    return pl.pallas_call(
        functools.partial(_combine_kernel, final_norm=final_norm),
        grid=(t // tm,),
        in_specs=[
            pl.BlockSpec((1, tm), lambda i: (0, i), memory_space=pltpu.SMEM),
            pl.BlockSpec((tm, D_MODEL), lambda i: (i, 0)),
            pl.BlockSpec(memory_space=pl.ANY),
            pl.BlockSpec((1, D_MODEL), lambda i: (0, 0)),
        ],
        out_specs=pl.BlockSpec((tm, D_MODEL), lambda i: (i, 0)),
        out_shape=jax.ShapeDtypeStruct((t, D_MODEL), F32),
        scratch_shapes=[pltpu.VMEM((tm, D_MODEL), F32), pltpu.SemaphoreType.DMA(())],
        compiler_params=_params(("arbitrary",)),
        name="combine",
    )(pos, x_mid, y, final_scale)


def _moe(x_mid, h2c, meta, counts, wg, wu, wd, final_scale, final_norm):
    t = x_mid.shape[0]
    tg = MOE_GROUP_TILE
    n_rows = t + N_GROUPS * tg
    cnt = counts[0, :N_GROUPS].astype(jnp.int32)
    padded = (cnt + tg - 1) // tg * tg
    ends = jnp.cumsum(padded)
    starts = ends - padded
    pos = (starts[meta[:, 0].astype(jnp.int32)] + meta[:, 1].astype(jnp.int32)).reshape(1, t)
    tile_start = jnp.arange(n_rows // tg, dtype=jnp.int32) * tg
    tile_group = jnp.minimum(jnp.sum(tile_start[:, None] >= ends[None, :], axis=1), N_GROUPS - 1)
    n_used = (ends[-1] // tg).reshape(1)
    xs = _dispatch(h2c, pos, n_rows)
    y = _experts(xs, tile_group.astype(jnp.int32), n_used.astype(jnp.int32), wg, wu, wd)
    return _combine(x_mid, y, pos, final_scale, final_norm)


def _rotary_tables(seq):
    inv = ROPE_THETA ** (-jnp.arange(0, HEAD_DIM, 2, dtype=F32) / HEAD_DIM)
    ang = jnp.arange(seq, dtype=F32)[:, None] * inv[None, :]
    cos, sin = jnp.cos(ang), jnp.sin(ang)
    return jnp.tile(cos, (1, 4)), jnp.concatenate([-sin, sin, -sin, sin], axis=1)


def _block_diag(w):
    nb = w.shape[0] // 4
    w4 = w.reshape(nb, 4, LRU_BLOCK, LRU_BLOCK)
    eye = jnp.eye(4, dtype=w.dtype)
    return jnp.einsum('qaij,ab->qaibj', w4, eye).reshape(nb, 4 * LRU_BLOCK, 4 * LRU_BLOCK)


def kernel(x, norm1_scale, w_in, b_merge, conv_w, conv_b, lru_wa, lru_ba, lru_wx, lru_bx, lru_lambda,
           w_proj_attn, w_proj_lru, w_out, norm2_scale, router_group_w, router_group_b, router_expert_w,
           router_expert_b, expert_w_gate, expert_w_up, expert_w_down, final_norm_scale):
    batch, seq, _ = x.shape
    depth = w_in.shape[0]
    cos_t, sin_t = _rotary_tables(seq)
    expand = (jnp.arange(LANES)[:, None] == (jnp.arange(GROUP_WIDTH)[None, :] // HEAD_DIM)).astype(BF16)
    expand = jnp.concatenate([expand, expand], axis=0)
    row = lambda v: v.reshape(1, -1)
    x2 = x.reshape(batch * seq, D_MODEL)
    for l in range(depth):
        *qkv, ug, gl = _in_proj(x2, row(norm1_scale[l]), w_in[l].astype(BF16), cos_t, sin_t, batch, seq)
        o_list, lse_list = [], []
        for g in range(len(DILATIONS)):
            o, lse = _attention_group(qkv[g], qkv[3 + g], qkv[6 + g], g)
            o_list.append(o)
            lse_list.append(lse)
        lru_out = _lru_branch(ug, conv_w[l], row(conv_b[l]), _block_diag(lru_wa[l]).astype(BF16),
                              _block_diag(lru_wx[l]).astype(BF16), row(lru_ba[l]), row(lru_bx[l]),
                              row(lru_lambda[l]), batch, seq)
        rw = jnp.zeros((D_MODEL, LANES), F32)
        rw = rw.at[:, :N_EXPERTS].set(router_expert_w[l]).at[:, N_EXPERTS:N_EXPERTS + N_GROUPS].set(
            router_group_w[l])
        rb = jnp.zeros((1, LANES), F32)
        rb = rb.at[0, :N_EXPERTS].set(router_expert_b[l]).at[0, N_EXPERTS:N_EXPERTS + N_GROUPS].set(
            router_group_b[l])
        rw_hi = rw.astype(BF16)
        rw_split = jnp.concatenate([rw_hi, (rw - rw_hi.astype(F32)).astype(BF16)], axis=1)
        x_mid, h2c, meta, counts = _mix(x2, o_list, lse_list, lru_out, gl, row(b_merge[l]),
                                        w_proj_attn[l].astype(BF16), w_proj_lru[l].astype(BF16),
                                        w_out[l].astype(BF16), row(norm2_scale[l]), expand, rw_split, rb, seq)
        x2 = _moe(x_mid, h2c, meta, counts, expert_w_gate[l].astype(BF16), expert_w_up[l].astype(BF16),
                  expert_w_down[l].astype(BF16), row(final_norm_scale), final_norm=(l == depth - 1))
    return x2.reshape(batch, seq, D_MODEL)
```

```python
import functools

import jax
import jax.numpy as jnp
from jax import lax
from jax.experimental import pallas as pl
from jax.experimental.pallas import tpu as pltpu

D_MODEL = 1024
HEAD_DIM = 64
GROUP_WIDTH = 512
DILATIONS = (1, 4, 16)
WINDOW_STEPS = 128
ATTN_WIDTH = 3 * GROUP_WIDTH
LRU_WIDTH = 1024
LRU_BLOCK = 64
N_IN = 3 * ATTN_WIDTH + 2 * LRU_WIDTH + 2 * D_MODEL
CONV_WIDTH = 4
LRU_C = 8.0
N_GROUPS = 4
EXPERTS_PER_GROUP = 4
N_EXPERTS = 16
D_EXPERT = 512
EPS = 1e-6
ROPE_THETA = 10000.0
LANES = 128
VMEM_LIMIT = 56 * 1024 * 1024
IN_PROJ_TILE = 512
ATTN_QUERY_BLOCK = 128
MOE_ROW = D_MODEL + LANES
MOE_TOKEN_TILE = 512
MOE_GROUP_TILE = 512

F32 = jnp.float32
BF16 = jnp.bfloat16


def _sigmoid(x):
    return 0.5 * jnp.tanh(0.5 * x) + 0.5


def _params(semantics):
    return pltpu.CompilerParams(dimension_semantics=semantics, vmem_limit_bytes=VMEM_LIMIT)


def _lru_gates(u, c, lru, ext, a_s, b_s):
    cw_ref, cb_ref, wa_ref, wx_ref, ba_ref, bx_ref, lam_ref = lru
    tm, width = u.shape
    ch = slice(c * width, (c + 1) * width)
    sub = 8
    ext[c, sub:, :] = u
    conv = cb_ref[:, ch] + cw_ref[0:1, ch] * ext[c, sub - 3:sub - 3 + tm, :]
    for j in range(1, CONV_WIDTH):
        conv = conv + cw_ref[j:j + 1, ch] * ext[c, sub - 3 + j:sub - 3 + j + tm, :]
    ext[c, 0:sub, :] = u[tm - sub:, :]
    xb = conv.astype(BF16)
    blk = 4 * LRU_BLOCK
    nb = width // blk
    ra = jnp.concatenate([jnp.dot(xb[:, q * blk:(q + 1) * blk], wa_ref[c * nb + q], preferred_element_type=F32)
                          for q in range(nb)], axis=1)
    rx = jnp.concatenate([jnp.dot(xb[:, q * blk:(q + 1) * blk], wx_ref[c * nb + q], preferred_element_type=F32)
                          for q in range(nb)], axis=1)
    r = _sigmoid(ra + ba_ref[:, ch])
    gate_i = _sigmoid(rx + bx_ref[:, ch])
    z = -lam_ref[:, ch]
    softplus = jnp.maximum(z, 0.0) + jnp.log1p(jnp.exp(-jnp.abs(z)))
    a = jnp.exp((-LRU_C) * r * softplus)
    a_s[c] = a
    one_m_a2 = 1.0 - a * a
    root = jnp.where(one_m_a2 > 0.0, one_m_a2 * lax.rsqrt(one_m_a2), 0.0)
    b_s[c] = root * (gate_i * conv)


def _lru_scan(c, a_s, b_s, carry):
    sub = 8
    tm, width = a_s.shape[1:]
    row = lax.broadcasted_iota(jnp.int32, (sub, width), 0)
    h_prev = carry[c]
    for k in range(tm // sub):
        rows = slice(k * sub, (k + 1) * sub)
        a8 = a_s[c, rows, :]
        b8 = b_s[c, rows, :]
        for sh in (1, 2, 4):
            keep = row >= sh
            a_sh = jnp.where(keep, pltpu.roll(a8, sh, 0), 1.0)
            b_sh = jnp.where(keep, pltpu.roll(b8, sh, 0), 0.0)
            b8 = a8 * b_sh + b8
            a8 = a8 * a_sh
        h8 = b8 + a8 * h_prev
        b_s[c, rows, :] = h8
        h_prev = jnp.broadcast_to(h8[sub - 1:sub, :], (sub, width))
    carry[c] = h_prev


def _in_proj_kernel(x_ref, scale_ref, w_ref, cos_ref, sin_ref, *rest, n_seq_tiles):
    lru, qkv_refs = rest[:7], rest[7:16]
    lru_ref, gl_ref, h32_ref, hn_ref, ext, a_s, b_s, gg_s, carry = rest[16:]
    half = LRU_WIDTH // 2

    @pl.when(pl.program_id(0) % n_seq_tiles == 0)
    def _():
        ext[:, 0:8, :] = jnp.zeros((2, 8, half), F32)
        carry[...] = jnp.zeros(carry.shape, F32)

    x = x_ref[...]
    tm = x.shape[0]
    ms = jnp.mean(x * x, axis=-1, keepdims=True)
    hn = x * lax.rsqrt(ms + EPS) * scale_ref[...]
    hn_ref[0] = hn.astype(BF16)
    n_lane_blocks = D_MODEL // LANES
    for j in range(n_lane_blocks):
        h32_ref[j] = hn[:, j * LANES:(j + 1) * LANES]
    for g in (1, 2):
        d = DILATIONS[g]
        rows = tm // d
        for r in range(d):
            for j in range(n_lane_blocks):
                hn_ref[g, r * rows:(r + 1) * rows, j * LANES:(j + 1) * LANES] = (
                    h32_ref[j, pl.ds(r, rows, stride=d), :].astype(BF16))
    lane = lax.broadcasted_iota(jnp.int32, (tm, LANES), 1)
    first_half = (lane % HEAD_DIM) < (HEAD_DIM // 2)
    n_chunks = N_IN // GROUP_WIDTH
    for c in (9, 10, 11, 12) + tuple(range(9)) + tuple(range(13, n_chunks)):
        g = c % 3 if c < 9 else 0
        res = jnp.dot(hn_ref[g], w_ref[:, c * GROUP_WIDTH:(c + 1) * GROUP_WIDTH],
                      preferred_element_type=F32)
        if c < 6:
            d = DILATIONS[g]
            cos, sin = (jnp.concatenate([t_ref[pl.ds(r, tm // d, stride=d), :] for r in range(d)], axis=0)
                        if d > 1 else t_ref[...] for t_ref in (cos_ref, sin_ref))
            parts = []
            for s in range(GROUP_WIDTH // LANES):
                blk = res[:, s * LANES:(s + 1) * LANES]
                partner = jnp.where(first_half, pltpu.roll(blk, LANES - HEAD_DIM // 2, 1),
                                    pltpu.roll(blk, HEAD_DIM // 2, 1))
                o = blk * cos + partner * sin
                if c < 3:
                    o = o * (HEAD_DIM ** -0.5)
                parts.append(o)
            res = jnp.concatenate(parts, axis=1)
        if c < 9:
            d = DILATIONS[g]
            qkv_refs[c][...] = res.reshape(d, tm // d, GROUP_WIDTH).astype(BF16)
        elif c < 11:
            _lru_gates(res, c - 9, lru, ext, a_s, b_s)
        elif c < 13:
            gg_s[c - 11] = _gelu_tanh(res).astype(BF16)
        else:
            gl_ref[:, (c - 13) * GROUP_WIDTH:(c - 12) * GROUP_WIDTH] = res.astype(BF16)
    for c in range(2):
        _lru_scan(c, a_s, b_s, carry)
        lru_ref[:, c * half:(c + 1) * half] = (gg_s[c].astype(F32) * b_s[c]).astype(BF16)


def _in_proj(x2, scale, w_bf, cos_t, sin_t, lru_params, batch, seq):
    t = x2.shape[0]
    tm = IN_PROJ_TILE
    n_seq_tiles = seq // tm
    half = LRU_WIDTH // 2
    full = lambda a: pl.BlockSpec(a.shape, lambda i: (0,) * a.ndim)
    qkv_specs, qkv_shapes = [], []
    for c in range(9):
        d = DILATIONS[c % 3]
        qkv_specs.append(pl.BlockSpec((None, d, tm // d, GROUP_WIDTH),
                                      lambda i: (i // n_seq_tiles, 0, i % n_seq_tiles, 0)))
        qkv_shapes.append(jax.ShapeDtypeStruct((batch, d, seq // d, GROUP_WIDTH), BF16))
    return pl.pallas_call(
        functools.partial(_in_proj_kernel, n_seq_tiles=n_seq_tiles),
        grid=(t // tm,),
        in_specs=[
            pl.BlockSpec((tm, D_MODEL), lambda i: (i, 0)),
            pl.BlockSpec((1, D_MODEL), lambda i: (0, 0)),
            pl.BlockSpec((D_MODEL, N_IN), lambda i: (0, 0), pipeline_mode=pl.Buffered(1)),
            pl.BlockSpec((tm, LANES), lambda i: (i % n_seq_tiles, 0)),
            pl.BlockSpec((tm, LANES), lambda i: (i % n_seq_tiles, 0)),
        ] + [full(a) for a in lru_params],
        out_specs=qkv_specs + [
            pl.BlockSpec((tm, LRU_WIDTH), lambda i: (i, 0)),
            pl.BlockSpec((tm, 2 * D_MODEL), lambda i: (i, 0)),
        ],
        out_shape=qkv_shapes + [
            jax.ShapeDtypeStruct((t, LRU_WIDTH), BF16),
            jax.ShapeDtypeStruct((t, 2 * D_MODEL), BF16),
        ],
        scratch_shapes=[
            pltpu.VMEM((D_MODEL // LANES, tm, LANES), F32),
            pltpu.VMEM((3, tm, D_MODEL), BF16),
            pltpu.VMEM((2, tm + 8, half), F32),
            pltpu.VMEM((2, tm, half), F32),
            pltpu.VMEM((2, tm, half), F32),
            pltpu.VMEM((2, tm, half), BF16),
            pltpu.VMEM((2, 8, half), F32),
        ],
        compiler_params=_params(("arbitrary",)),
        name="in_proj",
    )(x2, scale, w_bf, cos_t, sin_t, *lru_params)


def _attn_kernel(q_ref, kp_ref, kc_ref, vp_ref, vc_ref, o_ref, lse_ref, kcat, vcat, *, tq):
    w = WINDOW_STEPS
    qb = ATTN_QUERY_BLOCK
    kcat[0:w, :] = kp_ref[...]
    kcat[w:w + tq, :] = kc_ref[...]
    kcat[w + tq:, :] = jnp.zeros((qb, GROUP_WIDTH), BF16)
    vcat[0:w, :] = vp_ref[...]
    vcat[w:w + tq, :] = vc_ref[...]
    vcat[w + tq:, :] = jnp.zeros((qb, GROUP_WIDTH), BF16)
    first_tile = pl.program_id(2) == 0

    qi = lax.broadcasted_iota(jnp.int32, (2 * qb, 2 * w), 0) % qb
    col = lax.broadcasted_iota(jnp.int32, (2 * qb, 2 * w), 1)
    band = (col >= qi) & (col <= qi + w)
    lane_q = lax.broadcasted_iota(jnp.int32, (qb, LANES), 1)
    low_q = lane_q < HEAD_DIM

    n_pairs = GROUP_WIDTH // LANES

    def query_block(s, carry):
        r0 = pl.multiple_of(s * qb, qb)
        min_col = jnp.where(first_tile, w - r0, 0)
        valid = band & (col >= min_col)
        lse_tile = jnp.zeros((qb, LANES), F32)
        for p in range(n_pairs):
            lanes = slice(p * LANES, (p + 1) * LANES)
            qp = q_ref[pl.ds(r0, qb), lanes]
            zero = jnp.zeros_like(qp)
            qs = jnp.concatenate([jnp.where(low_q, qp, zero), jnp.where(low_q, zero, qp)], axis=0)
            kk = kcat[pl.ds(r0, 2 * w), lanes]
            vv = vcat[pl.ds(r0, 2 * w), lanes]
            sc = lax.dot_general(qs, kk, (((1,), (1,)), ((), ())), preferred_element_type=F32)
            sc = jnp.where(valid, sc, -jnp.inf)
            m = jnp.max(sc, axis=1, keepdims=True)
            pr = jnp.exp(sc - m)
            den = jnp.sum(pr, axis=1, keepdims=True)
            pv = jnp.dot(pr.astype(BF16), vv, preferred_element_type=F32) / den
            o_ref[pl.ds(r0, qb), lanes] = jnp.where(low_q, pv[:qb], pv[qb:]).astype(BF16)
            lse = m + jnp.log(den)
            lse_tile = jnp.where(lane_q == 2 * p, lse[:qb], lse_tile)
            lse_tile = jnp.where(lane_q == 2 * p + 1, lse[qb:], lse_tile)
        lse_ref[pl.ds(r0, qb), :] = lse_tile
        return carry

    lax.fori_loop(0, tq // qb, query_block, 0)


def _attention_group(q, k, v, group):
    batch, d, length, _ = q.shape
    tq = 512
    w = WINDOW_STEPS
    sub = tq // w
    cur = pl.BlockSpec((None, None, tq, GROUP_WIDTH), lambda b, r, i: (b, r, i, 0))
    prev = pl.BlockSpec((None, None, w, GROUP_WIDTH),
                        lambda b, r, i: (b, r, jnp.maximum(i * sub - 1, 0), 0))
    return pl.pallas_call(
        functools.partial(_attn_kernel, tq=tq),
        grid=(batch, d, length // tq),
        in_specs=[cur, prev, cur, prev, cur],
        out_specs=[
            pl.BlockSpec((None, None, tq, GROUP_WIDTH), lambda b, r, i: (b, r, i, 0)),
            pl.BlockSpec((None, None, tq, LANES), lambda b, r, i: (b, r, i, 0)),
        ],
        out_shape=[
            jax.ShapeDtypeStruct((batch, d, length, GROUP_WIDTH), BF16),
            jax.ShapeDtypeStruct((batch, d, length, LANES), F32),
        ],
        scratch_shapes=[pltpu.VMEM((tq + w + ATTN_QUERY_BLOCK, GROUP_WIDTH), BF16)] * 2,
        compiler_params=_params(("arbitrary", "arbitrary", "arbitrary")),
        name=f"attn_d{d}",
    )(q, k, k, v, v)


def _gelu_tanh(x):
    return 0.5 * x * (1.0 + jnp.tanh(0.7978845608028654 * (x + 0.044715 * (x * x * x))))


def _split_bf16(x):
    hi = x.astype(BF16)
    lo = (x - hi.astype(F32)).astype(BF16)
    return hi, lo


def _mix_kernel(x_ref, o1_ref, o2_ref, o3_ref, l1_ref, l2_ref, l3_ref, lru_ref, gl_ref, bm_ref,
                wpa_ref, wpl_ref, wo_ref, n2_ref, expand_ref, rw_ref, rb_ref, tri_ref,
                xo_ref, h2c_ref, meta_ref, cnt_ref, o2_s, o3_s, l2_s, l3_s, cnt_s):
    tm = x_ref.shape[0]

    @pl.when(pl.program_id(0) == 0)
    def _():
        cnt_s[...] = jnp.zeros(cnt_s.shape, F32)

    for src_o, src_l, dst_o, dst_l in ((o2_ref, l2_ref, o2_s, l2_s), (o3_ref, l3_ref, o3_s, l3_s)):
        d = src_o.shape[0]
        for r in range(d):
            o_r = src_o[r].astype(F32)
            for j in range(GROUP_WIDTH // LANES):
                dst_o[j, pl.ds(r, tm // d, stride=d), :] = o_r[:, j * LANES:(j + 1) * LANES]
            dst_l[pl.ds(r, tm // d, stride=d), :] = src_l[r]
    tok = lambda s: jnp.concatenate([s[j] for j in range(GROUP_WIDTH // LANES)], axis=1)
    l1, l2, l3 = l1_ref[0], l2_s[...], l3_s[...]
    mx = jnp.maximum(jnp.maximum(l1, l2), l3)
    e1, e2, e3 = jnp.exp(l1 - mx), jnp.exp(l2 - mx), jnp.exp(l3 - mx)
    inv = 1.0 / (e1 + e2 + e3)
    attn = None
    for e, o_tok in ((e1, o1_ref[0].astype(F32)), (e2, tok(o2_s)), (e3, tok(o3_s))):
        hi, lo = _split_bf16(e * inv)
        wide = jnp.dot(jnp.concatenate([hi, lo], axis=1), expand_ref[...], preferred_element_type=F32)
        term = wide * o_tok
        attn = term if attn is None else attn + term
    pa = jnp.dot(attn.astype(BF16), wpa_ref[...], preferred_element_type=F32)
    pr = jnp.dot(lru_ref[...], wpl_ref[...], preferred_element_type=F32)
    gates = _sigmoid(gl_ref[...].astype(F32) + bm_ref[...])
    mixed = gates[:, :D_MODEL] * pa + gates[:, D_MODEL:] * pr
    x = x_ref[...] + jnp.dot(mixed.astype(BF16), wo_ref[...], preferred_element_type=F32)
    xo_ref[...] = x

    ms = jnp.mean(x * x, axis=-1, keepdims=True)
    h2 = x * lax.rsqrt(ms + EPS) * n2_ref[...]
    h_hi, h_lo = _split_bf16(h2)
    both = jnp.dot(h_hi, rw_ref[...], preferred_element_type=F32)
    logits = (both[:, :LANES] + both[:, LANES:]
              + jnp.dot(h_lo, rw_ref[:, :LANES], preferred_element_type=F32)) + rb_ref[...]
    comb, g_idx = _route(logits)
    h2c_ref[:, :D_MODEL] = h2
    h2c_ref[:, D_MODEL:] = comb

    lane = lax.broadcasted_iota(jnp.int32, (tm, LANES), 1)
    onehot = lane == g_idx
    before = jnp.dot(tri_ref[...], onehot.astype(BF16), preferred_element_type=F32) + cnt_s[0:1, :]
    rank = jnp.sum(jnp.where(onehot, before, 0.0), axis=1, keepdims=True)
    meta_ref[...] = jnp.where(lane == 0, g_idx.astype(F32), jnp.where(lane == 1, rank, 0.0))
    cnt_s[...] = cnt_s[...] + jnp.sum(onehot.astype(F32), axis=0, keepdims=True)
    cnt_ref[...] = cnt_s[...]


def _route(logits):
    tm = logits.shape[0]
    lane = lax.broadcasted_iota(jnp.int32, (tm, LANES), 1)
    neg = -jnp.inf
    big = jnp.int32(LANES)
    is_g = (lane >= N_EXPERTS) & (lane < N_EXPERTS + N_GROUPS)
    gl = jnp.where(is_g, logits, neg)
    gmax = jnp.max(gl, axis=1, keepdims=True)
    g_idx = jnp.min(jnp.where(gl == gmax, lane, big), axis=1, keepdims=True) - N_EXPERTS
    g_w = 1.0 / jnp.sum(jnp.exp(gl - gmax), axis=1, keepdims=True)
    lo = g_idx * EXPERTS_PER_GROUP
    in_grp = (lane >= lo) & (lane < lo + EXPERTS_PER_GROUP)
    el = jnp.where(in_grp, logits, neg)
    v1 = jnp.max(el, axis=1, keepdims=True)
    i1 = jnp.min(jnp.where(el == v1, lane, big), axis=1, keepdims=True)
    el2 = jnp.where(lane == i1, neg, el)
    v2 = jnp.max(el2, axis=1, keepdims=True)
    i2 = jnp.min(jnp.where(el2 == v2, lane, big), axis=1, keepdims=True)
    e2 = jnp.exp(v2 - v1)
    w1 = g_w / (1.0 + e2)
    w2 = g_w * e2 / (1.0 + e2)
    return jnp.where(lane == i1, w1, 0.0) + jnp.where(lane == i2, w2, 0.0), g_idx


def _mix(x2, o_list, lse_list, lru_out, gl, b_merge, wpa, wpl, wo, norm2, expand, rw_split, rb, seq):
    t = x2.shape[0]
    tm = IN_PROJ_TILE
    n_seq_tiles = seq // tm
    row = lambda width: pl.BlockSpec((tm, width), lambda i: (i, 0))
    full = lambda a: pl.BlockSpec(a.shape, lambda i: (0,) * a.ndim)
    resid = lambda d, width: pl.BlockSpec((None, d, tm // d, width),
                                          lambda i: (i // n_seq_tiles, 0, i % n_seq_tiles, 0))
    tri = (jnp.arange(tm)[:, None] > jnp.arange(tm)[None, :]).astype(BF16)
    consts = [b_merge, wpa, wpl, wo, norm2, expand, rw_split, rb, tri]
    return pl.pallas_call(
        _mix_kernel,
        grid=(t // tm,),
        in_specs=[row(D_MODEL)] + [resid(d, GROUP_WIDTH) for d in DILATIONS]
                 + [resid(d, LANES) for d in DILATIONS]
                 + [row(LRU_WIDTH), row(2 * D_MODEL)] + [full(a) for a in consts],
        out_specs=[row(D_MODEL), row(MOE_ROW), row(LANES), pl.BlockSpec((8, LANES), lambda i: (0, 0))],
        out_shape=[
            jax.ShapeDtypeStruct((t, D_MODEL), F32),
            jax.ShapeDtypeStruct((t, MOE_ROW), F32),
            jax.ShapeDtypeStruct((t, LANES), F32),
            jax.ShapeDtypeStruct((8, LANES), F32),
        ],
        scratch_shapes=[pltpu.VMEM((GROUP_WIDTH // LANES, tm, LANES), F32),
                        pltpu.VMEM((GROUP_WIDTH // LANES, tm, LANES), F32),
                        pltpu.VMEM((tm, LANES), F32), pltpu.VMEM((tm, LANES), F32),
                        pltpu.VMEM((8, LANES), F32)],
        compiler_params=_params(("arbitrary",)),
        name="mix",
    )(x2, *o_list, *lse_list, lru_out, gl, *consts)


def _start_row_copies(n_rows, make_copy):
    for r in range(n_rows):
        make_copy(r).start(priority=r % 2)


def _dispatch_kernel(pos_ref, h_ref, xs_init_ref, xs_ref, sem):
    del xs_init_ref
    tm = h_ref.shape[0]
    _start_row_copies(tm, lambda r: pltpu.make_async_copy(
        h_ref.at[pl.ds(r, 1), :], xs_ref.at[pl.ds(pos_ref[0, r], 1), :], sem))
    pltpu.make_async_copy(h_ref, xs_ref.at[pl.ds(0, tm), :], sem).wait()


def _dispatch(h2c, pos, n_rows):
    t = h2c.shape[0]
    tm = MOE_TOKEN_TILE
    return pl.pallas_call(
        _dispatch_kernel,
        grid=(t // tm,),
        in_specs=[
            pl.BlockSpec((1, tm), lambda i: (0, i), memory_space=pltpu.SMEM),
            pl.BlockSpec((tm, MOE_ROW), lambda i: (i, 0)),
            pl.BlockSpec(memory_space=pl.ANY),
        ],
        out_specs=pl.BlockSpec(memory_space=pl.ANY),
        out_shape=jax.ShapeDtypeStruct((n_rows, MOE_ROW), F32),
        scratch_shapes=[pltpu.SemaphoreType.DMA(())],
        input_output_aliases={2: 0},
        compiler_params=_params(("arbitrary",)),
        name="dispatch",
    )(pos, h2c, jnp.zeros((n_rows, MOE_ROW), F32))


def _expert_kernel(tile_group_ref, n_used_ref, xs_ref, wg_ref, wu_ref, wd_ref, y_ref):
    j = pl.program_id(0)

    @pl.when(j < n_used_ref[0])
    def _():
        h = xs_ref[:, :D_MODEL].astype(BF16)
        comb = xs_ref[:, D_MODEL:]
        lane = lax.broadcasted_iota(jnp.int32, comb.shape, 1)
        first = tile_group_ref[j] * EXPERTS_PER_GROUP
        y = None
        for e in range(EXPERTS_PER_GROUP):
            gate = jnp.dot(h, wg_ref[e], preferred_element_type=F32)
            up = jnp.dot(h, wu_ref[e], preferred_element_type=F32)
            w_e = jnp.sum(jnp.where(lane == first + e, comb, 0.0), axis=1, keepdims=True)
            act = (gate * _sigmoid(gate) * up * w_e).astype(BF16)
            part = jnp.dot(act, wd_ref[e], preferred_element_type=F32)
            y = part if y is None else y + part
        y_ref[...] = y

    @pl.when(j >= n_used_ref[0])
    def _():
        y_ref[...] = jnp.zeros(y_ref.shape, F32)


def _experts(xs, tile_group, n_used, wg, wu, wd):
    n_rows = xs.shape[0]
    tg = MOE_GROUP_TILE
    epg = EXPERTS_PER_GROUP
    return pl.pallas_call(
        _expert_kernel,
        grid_spec=pltpu.PrefetchScalarGridSpec(
            num_scalar_prefetch=2,
            grid=(n_rows // tg,),
            in_specs=[
                pl.BlockSpec((tg, MOE_ROW), lambda j, tgr, nu: (j, 0)),
                pl.BlockSpec((epg, D_MODEL, D_EXPERT), lambda j, tgr, nu: (tgr[j], 0, 0)),
                pl.BlockSpec((epg, D_MODEL, D_EXPERT), lambda j, tgr, nu: (tgr[j], 0, 0)),
                pl.BlockSpec((epg, D_EXPERT, D_MODEL), lambda j, tgr, nu: (tgr[j], 0, 0)),
            ],
            out_specs=pl.BlockSpec((tg, D_MODEL), lambda j, tgr, nu: (j, 0)),
        ),
        out_shape=jax.ShapeDtypeStruct((n_rows, D_MODEL), F32),
        compiler_params=_params(("arbitrary",)),
        name="experts",
    )(tile_group, n_used, xs, wg, wu, wd)


def _combine_kernel(pos_ref, x_ref, y_ref, fn_ref, out_ref, ybuf, sem, *, final_norm):
    tm = ybuf.shape[0]
    _start_row_copies(tm, lambda r: pltpu.make_async_copy(
        y_ref.at[pl.ds(pos_ref[0, r], 1), :], ybuf.at[pl.ds(r, 1), :], sem))
    pltpu.make_async_copy(y_ref.at[pl.ds(0, tm), :], ybuf, sem).wait()
    x = x_ref[...] + ybuf[...]
    if final_norm:
        ms = jnp.mean(x * x, axis=-1, keepdims=True)
        x = x * lax.rsqrt(ms + EPS) * fn_ref[...]
    out_ref[...] = x


def _combine(x_mid, y, pos, final_scale, final_norm):
    t = x_mid.shape[0]
    tm = MOE_TOKEN_TILE
    return pl.pallas_call(
        functools.partial(_combine_kernel, final_norm=final_norm),
        grid=(t // tm,),
        in_specs=[
            pl.BlockSpec((1, tm), lambda i: (0, i), memory_space=pltpu.SMEM),
            pl.BlockSpec((tm, D_MODEL), lambda i: (i, 0)),
            pl.BlockSpec(memory_space=pl.ANY),
            pl.BlockSpec((1, D_MODEL), lambda i: (0, 0)),
        ],
        out_specs=pl.BlockSpec((tm, D_MODEL), lambda i: (i, 0)),
        out_shape=jax.ShapeDtypeStruct((t, D_MODEL), F32),
        scratch_shapes=[pltpu.VMEM((tm, D_MODEL), F32), pltpu.SemaphoreType.DMA(())],
        compiler_params=_params(("arbitrary",)),
        name="combine",
    )(pos, x_mid, y, final_scale)


def _moe(x_mid, h2c, meta, counts, wg, wu, wd, final_scale, final_norm):
    t = x_mid.shape[0]
    tg = MOE_GROUP_TILE
    n_rows = t + N_GROUPS * tg
    cnt = counts[0, :N_GROUPS].astype(jnp.int32)
    padded = (cnt + tg - 1) // tg * tg
    ends = jnp.cumsum(padded)
    starts = ends - padded
    pos = (starts[meta[:, 0].astype(jnp.int32)] + meta[:, 1].astype(jnp.int32)).reshape(1, t)
    tile_start = jnp.arange(n_rows // tg, dtype=jnp.int32) * tg
    tile_group = jnp.minimum(jnp.sum(tile_start[:, None] >= ends[None, :], axis=1), N_GROUPS - 1)
    n_used = (ends[-1] // tg).reshape(1)
    xs = _dispatch(h2c, pos, n_rows)
    y = _experts(xs, tile_group.astype(jnp.int32), n_used.astype(jnp.int32), wg, wu, wd)
    return _combine(x_mid, y, pos, final_scale, final_norm)


def _rotary_tables(seq):
    inv = ROPE_THETA ** (-jnp.arange(0, HEAD_DIM, 2, dtype=F32) / HEAD_DIM)
    ang = jnp.arange(seq, dtype=F32)[:, None] * inv[None, :]
    cos, sin = jnp.cos(ang), jnp.sin(ang)
    return jnp.tile(cos, (1, 4)), jnp.concatenate([-sin, sin, -sin, sin], axis=1)


def _block_diag(w):
    nb = w.shape[0] // 4
    w4 = w.reshape(nb, 4, LRU_BLOCK, LRU_BLOCK)
    eye = jnp.eye(4, dtype=w.dtype)
    return jnp.einsum('qaij,ab->qaibj', w4, eye).reshape(nb, 4 * LRU_BLOCK, 4 * LRU_BLOCK)


def kernel(x, norm1_scale, w_in, b_merge, conv_w, conv_b, lru_wa, lru_ba, lru_wx, lru_bx, lru_lambda,
           w_proj_attn, w_proj_lru, w_out, norm2_scale, router_group_w, router_group_b, router_expert_w,
           router_expert_b, expert_w_gate, expert_w_up, expert_w_down, final_norm_scale):
    batch, seq, _ = x.shape
    depth = w_in.shape[0]
    cos_t, sin_t = _rotary_tables(seq)
    expand = (jnp.arange(LANES)[:, None] == (jnp.arange(GROUP_WIDTH)[None, :] // HEAD_DIM)).astype(BF16)
    expand = jnp.concatenate([expand, expand], axis=0)
    row = lambda v: v.reshape(1, -1)
    x2 = x.reshape(batch * seq, D_MODEL)
    for l in range(depth):
        lru_params = (conv_w[l], row(conv_b[l]), _block_diag(lru_wa[l]).astype(BF16),
                      _block_diag(lru_wx[l]).astype(BF16), row(lru_ba[l]), row(lru_bx[l]), row(lru_lambda[l]))
        *qkv, lru_out, gl = _in_proj(x2, row(norm1_scale[l]), w_in[l].astype(BF16), cos_t, sin_t, lru_params,
                                     batch, seq)
        o_list, lse_list = [], []
        for g in range(len(DILATIONS)):
            o, lse = _attention_group(qkv[g], qkv[3 + g], qkv[6 + g], g)
            o_list.append(o)
            lse_list.append(lse)
        rw = jnp.zeros((D_MODEL, LANES), F32)
        rw = rw.at[:, :N_EXPERTS].set(router_expert_w[l]).at[:, N_EXPERTS:N_EXPERTS + N_GROUPS].set(
            router_group_w[l])
        rb = jnp.zeros((1, LANES), F32)
        rb = rb.at[0, :N_EXPERTS].set(router_expert_b[l]).at[0, N_EXPERTS:N_EXPERTS + N_GROUPS].set(
            router_group_b[l])
        rw_hi = rw.astype(BF16)
        rw_split = jnp.concatenate([rw_hi, (rw - rw_hi.astype(F32)).astype(BF16)], axis=1)
        x_mid, h2c, meta, counts = _mix(x2, o_list, lse_list, lru_out, gl, row(b_merge[l]),
                                        w_proj_attn[l].astype(BF16), w_proj_lru[l].astype(BF16),
                                        w_out[l].astype(BF16), row(norm2_scale[l]), expand, rw_split, rb, seq)
        x2 = _moe(x_mid, h2c, meta, counts, expert_w_gate[l].astype(BF16), expert_w_up[l].astype(BF16),
                  expert_w_down[l].astype(BF16), row(final_norm_scale), final_norm=(l == depth - 1))
    return x2.reshape(batch, seq, D_MODEL)
```

```python
import functools

import jax
import jax.numpy as jnp
from jax import lax
from jax.experimental import pallas as pl
from jax.experimental.pallas import tpu as pltpu

D_MODEL = 1024
HEAD_DIM = 64
GROUP_WIDTH = 512
DILATIONS = (1, 4, 16)
WINDOW_STEPS = 128
ATTN_WIDTH = 3 * GROUP_WIDTH
LRU_WIDTH = 1024
LRU_BLOCK = 64
N_IN = 3 * ATTN_WIDTH + 2 * LRU_WIDTH + 2 * D_MODEL
CONV_WIDTH = 4
LRU_C = 8.0
N_GROUPS = 4
EXPERTS_PER_GROUP = 4
N_EXPERTS = 16
D_EXPERT = 512
EPS = 1e-6
ROPE_THETA = 10000.0
LANES = 128
VMEM_LIMIT = 56 * 1024 * 1024
IN_PROJ_TILE = 512
ATTN_QUERY_BLOCK = 128
MOE_ROW = D_MODEL + LANES
MOE_TOKEN_TILE = 512
MOE_GROUP_TILE = 512

F32 = jnp.float32
BF16 = jnp.bfloat16


def _sigmoid(x):
    return 0.5 * jnp.tanh(0.5 * x) + 0.5


def _params(semantics):
    return pltpu.CompilerParams(dimension_semantics=semantics, vmem_limit_bytes=VMEM_LIMIT)


def _lru_gates(u, c, lru, ext, a_s, b_s):
    cw_ref, cb_ref, wa_ref, wx_ref, ba_ref, bx_ref, lam_ref = lru
    tm, width = u.shape
    ch = slice(c * width, (c + 1) * width)
    sub = 8
    ext[c, sub:, :] = u
    conv = cb_ref[:, ch] + cw_ref[0:1, ch] * ext[c, sub - 3:sub - 3 + tm, :]
    for j in range(1, CONV_WIDTH):
        conv = conv + cw_ref[j:j + 1, ch] * ext[c, sub - 3 + j:sub - 3 + j + tm, :]
    ext[c, 0:sub, :] = u[tm - sub:, :]
    xb = conv.astype(BF16)
    blk = 4 * LRU_BLOCK
    nb = width // blk
    ra = jnp.concatenate([jnp.dot(xb[:, q * blk:(q + 1) * blk], wa_ref[c * nb + q], preferred_element_type=F32)
                          for q in range(nb)], axis=1)
    rx = jnp.concatenate([jnp.dot(xb[:, q * blk:(q + 1) * blk], wx_ref[c * nb + q], preferred_element_type=F32)
                          for q in range(nb)], axis=1)
    r = _sigmoid(ra + ba_ref[:, ch])
    gate_i = _sigmoid(rx + bx_ref[:, ch])
    z = -lam_ref[:, ch]
    softplus = jnp.maximum(z, 0.0) + jnp.log1p(jnp.exp(-jnp.abs(z)))
    a = jnp.exp((-LRU_C) * r * softplus)
    a_s[c] = a
    one_m_a2 = 1.0 - a * a
    root = jnp.where(one_m_a2 > 0.0, one_m_a2 * lax.rsqrt(one_m_a2), 0.0)
    b_s[c] = root * (gate_i * conv)


def _lru_scan(c, a_s, b_s, carry):
    sub = 8
    tm, width = a_s.shape[1:]
    row = lax.broadcasted_iota(jnp.int32, (sub, width), 0)
    h_prev = carry[c]
    for k in range(tm // sub):
        rows = slice(k * sub, (k + 1) * sub)
        a8 = a_s[c, rows, :]
        b8 = b_s[c, rows, :]
        for sh in (1, 2, 4):
            keep = row >= sh
            a_sh = jnp.where(keep, pltpu.roll(a8, sh, 0), 1.0)
            b_sh = jnp.where(keep, pltpu.roll(b8, sh, 0), 0.0)
            b8 = a8 * b_sh + b8
            a8 = a8 * a_sh
        h8 = b8 + a8 * h_prev
        b_s[c, rows, :] = h8
        h_prev = jnp.broadcast_to(h8[sub - 1:sub, :], (sub, width))
    carry[c] = h_prev


def _in_proj_kernel(x_ref, scale_ref, w_ref, cos_ref, sin_ref, *rest, n_seq_tiles):
    lru, qkv_refs = rest[:7], rest[7:16]
    lru_ref, gl_ref, h32_ref, hn_ref, ext, a_s, b_s, gg_s, carry = rest[16:]
    half = LRU_WIDTH // 2

    @pl.when(pl.program_id(0) % n_seq_tiles == 0)
    def _():
        ext[:, 0:8, :] = jnp.zeros((2, 8, half), F32)
        carry[...] = jnp.zeros(carry.shape, F32)

    x = x_ref[...]
    tm = x.shape[0]
    ms = jnp.mean(x * x, axis=-1, keepdims=True)
    hn = x * lax.rsqrt(ms + EPS) * scale_ref[...]
    hn_ref[0] = hn.astype(BF16)
    n_lane_blocks = D_MODEL // LANES
    for j in range(n_lane_blocks):
        h32_ref[j] = hn[:, j * LANES:(j + 1) * LANES]
    for g in (1, 2):
        d = DILATIONS[g]
        rows = tm // d
        for r in range(d):
            for j in range(n_lane_blocks):
                hn_ref[g, r * rows:(r + 1) * rows, j * LANES:(j + 1) * LANES] = (
                    h32_ref[j, pl.ds(r, rows, stride=d), :].astype(BF16))
    lane = lax.broadcasted_iota(jnp.int32, (tm, LANES), 1)
    first_half = (lane % HEAD_DIM) < (HEAD_DIM // 2)
    n_chunks = N_IN // GROUP_WIDTH
    for c in (9, 10, 11, 12) + tuple(range(9)) + tuple(range(13, n_chunks)):
        g = c % 3 if c < 9 else 0
        res = jnp.dot(hn_ref[g], w_ref[:, c * GROUP_WIDTH:(c + 1) * GROUP_WIDTH],
                      preferred_element_type=F32)
        if c < 6:
            d = DILATIONS[g]
            cos, sin = (jnp.concatenate([t_ref[pl.ds(r, tm // d, stride=d), :] for r in range(d)], axis=0)
                        if d > 1 else t_ref[...] for t_ref in (cos_ref, sin_ref))
            parts = []
            for s in range(GROUP_WIDTH // LANES):
                blk = res[:, s * LANES:(s + 1) * LANES]
                partner = jnp.where(first_half, pltpu.roll(blk, LANES - HEAD_DIM // 2, 1),
                                    pltpu.roll(blk, HEAD_DIM // 2, 1))
                o = blk * cos + partner * sin
                if c < 3:
                    o = o * (HEAD_DIM ** -0.5)
                parts.append(o)
            res = jnp.concatenate(parts, axis=1)
        if c < 9:
            d = DILATIONS[g]
            qkv_refs[c][...] = res.reshape(d, tm // d, GROUP_WIDTH).astype(BF16)
        elif c < 11:
            _lru_gates(res, c - 9, lru, ext, a_s, b_s)
        elif c < 13:
            gg_s[c - 11] = _gelu_tanh(res).astype(BF16)
        else:
            gl_ref[:, (c - 13) * GROUP_WIDTH:(c - 12) * GROUP_WIDTH] = res.astype(BF16)
    for c in range(2):
        _lru_scan(c, a_s, b_s, carry)
        lru_ref[:, c * half:(c + 1) * half] = (gg_s[c].astype(F32) * b_s[c]).astype(BF16)


def _in_proj(x2, scale, w_bf, cos_t, sin_t, lru_params, batch, seq):
    t = x2.shape[0]
    tm = IN_PROJ_TILE
    n_seq_tiles = seq // tm
    half = LRU_WIDTH // 2
    full = lambda a: pl.BlockSpec(a.shape, lambda i: (0,) * a.ndim)
    qkv_specs, qkv_shapes = [], []
    for c in range(9):
        d = DILATIONS[c % 3]
        qkv_specs.append(pl.BlockSpec((None, d, tm // d, GROUP_WIDTH),
                                      lambda i: (i // n_seq_tiles, 0, i % n_seq_tiles, 0)))
        qkv_shapes.append(jax.ShapeDtypeStruct((batch, d, seq // d, GROUP_WIDTH), BF16))
    return pl.pallas_call(
        functools.partial(_in_proj_kernel, n_seq_tiles=n_seq_tiles),
        grid=(t // tm,),
        in_specs=[
            pl.BlockSpec((tm, D_MODEL), lambda i: (i, 0)),
            pl.BlockSpec((1, D_MODEL), lambda i: (0, 0)),
            pl.BlockSpec((D_MODEL, N_IN), lambda i: (0, 0), pipeline_mode=pl.Buffered(1)),
            pl.BlockSpec((tm, LANES), lambda i: (i % n_seq_tiles, 0)),
            pl.BlockSpec((tm, LANES), lambda i: (i % n_seq_tiles, 0)),
        ] + [full(a) for a in lru_params],
        out_specs=qkv_specs + [
            pl.BlockSpec((tm, LRU_WIDTH), lambda i: (i, 0)),
            pl.BlockSpec((tm, 2 * D_MODEL), lambda i: (i, 0)),
        ],
        out_shape=qkv_shapes + [
            jax.ShapeDtypeStruct((t, LRU_WIDTH), BF16),
            jax.ShapeDtypeStruct((t, 2 * D_MODEL), BF16),
        ],
        scratch_shapes=[
            pltpu.VMEM((D_MODEL // LANES, tm, LANES), F32),
            pltpu.VMEM((3, tm, D_MODEL), BF16),
            pltpu.VMEM((2, tm + 8, half), F32),
            pltpu.VMEM((2, tm, half), F32),
            pltpu.VMEM((2, tm, half), F32),
            pltpu.VMEM((2, tm, half), BF16),
            pltpu.VMEM((2, 8, half), F32),
        ],
        compiler_params=_params(("arbitrary",)),
        name="in_proj",
    )(x2, scale, w_bf, cos_t, sin_t, *lru_params)


def _attn_kernel(q_ref, kp_ref, kc_ref, vp_ref, vc_ref, o_ref, lse_ref, kcat, vcat, *, tq):
    w = WINDOW_STEPS
    qb = ATTN_QUERY_BLOCK
    kcat[0:w, :] = kp_ref[...]
    kcat[w:w + tq, :] = kc_ref[...]
    kcat[w + tq:, :] = jnp.zeros((qb, GROUP_WIDTH), BF16)
    vcat[0:w, :] = vp_ref[...]
    vcat[w:w + tq, :] = vc_ref[...]
    vcat[w + tq:, :] = jnp.zeros((qb, GROUP_WIDTH), BF16)
    first_tile = pl.program_id(2) == 0

    qi = lax.broadcasted_iota(jnp.int32, (2 * qb, 2 * w), 0) % qb
    col = lax.broadcasted_iota(jnp.int32, (2 * qb, 2 * w), 1)
    band = (col >= qi) & (col <= qi + w)
    lane_q = lax.broadcasted_iota(jnp.int32, (qb, LANES), 1)
    low_q = lane_q < HEAD_DIM

    n_pairs = GROUP_WIDTH // LANES

    def query_block(s, carry):
        r0 = s * qb
        min_col = jnp.where(first_tile, w - r0, 0)
        valid = band & (col >= min_col)
        lse_tile = jnp.zeros((qb, LANES), F32)
        for p in range(n_pairs):
            lanes = slice(p * LANES, (p + 1) * LANES)
            qp = q_ref[pl.ds(r0, qb), lanes]
            zero = jnp.zeros_like(qp)
            qs = jnp.concatenate([jnp.where(low_q, qp, zero), jnp.where(low_q, zero, qp)], axis=0)
            kk = kcat[pl.ds(r0, 2 * w), lanes]
            vv = vcat[pl.ds(r0, 2 * w), lanes]
            sc = lax.dot_general(qs, kk, (((1,), (1,)), ((), ())), preferred_element_type=F32)
            sc = jnp.where(valid, sc, -jnp.inf)
            m = jnp.max(sc, axis=1, keepdims=True)
            pr = jnp.exp(sc - m)
            den = jnp.sum(pr, axis=1, keepdims=True)
            pv = jnp.dot(pr.astype(BF16), vv, preferred_element_type=F32) / den
            o_ref[pl.ds(r0, qb), lanes] = jnp.where(low_q, pv[:qb], pv[qb:]).astype(BF16)
            lse = m + jnp.log(den)
            lse_tile = jnp.where(lane_q == 2 * p, lse[:qb], lse_tile)
            lse_tile = jnp.where(lane_q == 2 * p + 1, lse[qb:], lse_tile)
        lse_ref[pl.ds(r0, qb), :] = lse_tile
        return carry

    for s in range(tq // qb):
        query_block(s, 0)


def _attention_group(q, k, v, group):
    batch, d, length, _ = q.shape
    tq = 1024
    w = WINDOW_STEPS
    sub = tq // w
    cur = pl.BlockSpec((None, None, tq, GROUP_WIDTH), lambda b, r, i: (b, r, i, 0))
    prev = pl.BlockSpec((None, None, w, GROUP_WIDTH),
                        lambda b, r, i: (b, r, jnp.maximum(i * sub - 1, 0), 0))
    return pl.pallas_call(
        functools.partial(_attn_kernel, tq=tq),
        grid=(batch, d, length // tq),
        in_specs=[cur, prev, cur, prev, cur],
        out_specs=[
            pl.BlockSpec((None, None, tq, GROUP_WIDTH), lambda b, r, i: (b, r, i, 0)),
            pl.BlockSpec((None, None, tq, LANES), lambda b, r, i: (b, r, i, 0)),
        ],
        out_shape=[
            jax.ShapeDtypeStruct((batch, d, length, GROUP_WIDTH), BF16),
            jax.ShapeDtypeStruct((batch, d, length, LANES), F32),
        ],
        scratch_shapes=[pltpu.VMEM((tq + w + ATTN_QUERY_BLOCK, GROUP_WIDTH), BF16)] * 2,
        compiler_params=_params(("arbitrary", "arbitrary", "arbitrary")),
        name=f"attn_d{d}",
    )(q, k, k, v, v)


def _gelu_tanh(x):
    return 0.5 * x * (1.0 + jnp.tanh(0.7978845608028654 * (x + 0.044715 * (x * x * x))))


def _split_bf16(x):
    hi = x.astype(BF16)
    lo = (x - hi.astype(F32)).astype(BF16)
    return hi, lo


def _mix_kernel(x_ref, o1_ref, o2_ref, o3_ref, l1_ref, l2_ref, l3_ref, lru_ref, gl_ref, bm_ref,
                wpa_ref, wpl_ref, wo_ref, n2_ref, expand_ref, rw_ref, rb_ref, tri_ref,
                xo_ref, h2c_ref, meta_ref, cnt_ref, o2_s, o3_s, l2_s, l3_s, cnt_s):
    tm = x_ref.shape[0]

    @pl.when(pl.program_id(0) == 0)
    def _():
        cnt_s[...] = jnp.zeros(cnt_s.shape, F32)

    for src_o, src_l, dst_o, dst_l in ((o2_ref, l2_ref, o2_s, l2_s), (o3_ref, l3_ref, o3_s, l3_s)):
        d = src_o.shape[0]
        for r in range(d):
            o_r = src_o[r].astype(F32)
            for j in range(GROUP_WIDTH // LANES):
                dst_o[j, pl.ds(r, tm // d, stride=d), :] = o_r[:, j * LANES:(j + 1) * LANES]
            dst_l[pl.ds(r, tm // d, stride=d), :] = src_l[r]
    tok = lambda s: jnp.concatenate([s[j] for j in range(GROUP_WIDTH // LANES)], axis=1)
    l1, l2, l3 = l1_ref[0], l2_s[...], l3_s[...]
    mx = jnp.maximum(jnp.maximum(l1, l2), l3)
    e1, e2, e3 = jnp.exp(l1 - mx), jnp.exp(l2 - mx), jnp.exp(l3 - mx)
    inv = 1.0 / (e1 + e2 + e3)
    attn = None
    for e, o_tok in ((e1, o1_ref[0].astype(F32)), (e2, tok(o2_s)), (e3, tok(o3_s))):
        hi, lo = _split_bf16(e * inv)
        wide = jnp.dot(jnp.concatenate([hi, lo], axis=1), expand_ref[...], preferred_element_type=F32)
        term = wide * o_tok
        attn = term if attn is None else attn + term
    pa = jnp.dot(attn.astype(BF16), wpa_ref[...], preferred_element_type=F32)
    pr = jnp.dot(lru_ref[...], wpl_ref[...], preferred_element_type=F32)
    gates = _sigmoid(gl_ref[...].astype(F32) + bm_ref[...])
    mixed = gates[:, :D_MODEL] * pa + gates[:, D_MODEL:] * pr
    x = x_ref[...] + jnp.dot(mixed.astype(BF16), wo_ref[...], preferred_element_type=F32)
    xo_ref[...] = x

    ms = jnp.mean(x * x, axis=-1, keepdims=True)
    h2 = x * lax.rsqrt(ms + EPS) * n2_ref[...]
    h_hi, h_lo = _split_bf16(h2)
    both = jnp.dot(h_hi, rw_ref[...], preferred_element_type=F32)
    logits = (both[:, :LANES] + both[:, LANES:]
              + jnp.dot(h_lo, rw_ref[:, :LANES], preferred_element_type=F32)) + rb_ref[...]
    comb, g_idx = _route(logits)
    h2c_ref[:, :D_MODEL] = h2
    h2c_ref[:, D_MODEL:] = comb

    lane = lax.broadcasted_iota(jnp.int32, (tm, LANES), 1)
    onehot = lane == g_idx
    before = jnp.dot(tri_ref[...], onehot.astype(BF16), preferred_element_type=F32) + cnt_s[0:1, :]
    rank = jnp.sum(jnp.where(onehot, before, 0.0), axis=1, keepdims=True)
    meta_ref[...] = jnp.where(lane == 0, g_idx.astype(F32), jnp.where(lane == 1, rank, 0.0))
    cnt_s[...] = cnt_s[...] + jnp.sum(onehot.astype(F32), axis=0, keepdims=True)
    cnt_ref[...] = cnt_s[...]


def _route(logits):
    tm = logits.shape[0]
    lane = lax.broadcasted_iota(jnp.int32, (tm, LANES), 1)
    neg = -jnp.inf
    big = jnp.int32(LANES)
    is_g = (lane >= N_EXPERTS) & (lane < N_EXPERTS + N_GROUPS)
    gl = jnp.where(is_g, logits, neg)
    gmax = jnp.max(gl, axis=1, keepdims=True)
    g_idx = jnp.min(jnp.where(gl == gmax, lane, big), axis=1, keepdims=True) - N_EXPERTS
    g_w = 1.0 / jnp.sum(jnp.exp(gl - gmax), axis=1, keepdims=True)
    lo = g_idx * EXPERTS_PER_GROUP
    in_grp = (lane >= lo) & (lane < lo + EXPERTS_PER_GROUP)
    el = jnp.where(in_grp, logits, neg)
    v1 = jnp.max(el, axis=1, keepdims=True)
    i1 = jnp.min(jnp.where(el == v1, lane, big), axis=1, keepdims=True)
    el2 = jnp.where(lane == i1, neg, el)
    v2 = jnp.max(el2, axis=1, keepdims=True)
    i2 = jnp.min(jnp.where(el2 == v2, lane, big), axis=1, keepdims=True)
    e2 = jnp.exp(v2 - v1)
    w1 = g_w / (1.0 + e2)
    w2 = g_w * e2 / (1.0 + e2)
    return jnp.where(lane == i1, w1, 0.0) + jnp.where(lane == i2, w2, 0.0), g_idx


def _mix(x2, o_list, lse_list, lru_out, gl, b_merge, wpa, wpl, wo, norm2, expand, rw_split, rb, seq):
    t = x2.shape[0]
    tm = IN_PROJ_TILE
    n_seq_tiles = seq // tm
    row = lambda width: pl.BlockSpec((tm, width), lambda i: (i, 0))
    full = lambda a: pl.BlockSpec(a.shape, lambda i: (0,) * a.ndim)
    resid = lambda d, width: pl.BlockSpec((None, d, tm // d, width),
                                          lambda i: (i // n_seq_tiles, 0, i % n_seq_tiles, 0))
    tri = (jnp.arange(tm)[:, None] > jnp.arange(tm)[None, :]).astype(BF16)
    consts = [b_merge, wpa, wpl, wo, norm2, expand, rw_split, rb, tri]
    return pl.pallas_call(
        _mix_kernel,
        grid=(t // tm,),
        in_specs=[row(D_MODEL)] + [resid(d, GROUP_WIDTH) for d in DILATIONS]
                 + [resid(d, LANES) for d in DILATIONS]
                 + [row(LRU_WIDTH), row(2 * D_MODEL)] + [full(a) for a in consts],
        out_specs=[row(D_MODEL), row(MOE_ROW), row(LANES), pl.BlockSpec((8, LANES), lambda i: (0, 0))],
        out_shape=[
            jax.ShapeDtypeStruct((t, D_MODEL), F32),
            jax.ShapeDtypeStruct((t, MOE_ROW), F32),
            jax.ShapeDtypeStruct((t, LANES), F32),
            jax.ShapeDtypeStruct((8, LANES), F32),
        ],
        scratch_shapes=[pltpu.VMEM((GROUP_WIDTH // LANES, tm, LANES), F32),
                        pltpu.VMEM((GROUP_WIDTH // LANES, tm, LANES), F32),
                        pltpu.VMEM((tm, LANES), F32), pltpu.VMEM((tm, LANES), F32),
                        pltpu.VMEM((8, LANES), F32)],
        compiler_params=_params(("arbitrary",)),
        name="mix",
    )(x2, *o_list, *lse_list, lru_out, gl, *consts)


def _start_row_copies(n_rows, make_copy):
    for r in range(n_rows):
        make_copy(r).start(priority=r % 2)


def _dispatch_kernel(pos_ref, h_ref, xs_init_ref, xs_ref, sem):
    del xs_init_ref
    tm = h_ref.shape[0]
    _start_row_copies(tm, lambda r: pltpu.make_async_copy(
        h_ref.at[pl.ds(r, 1), :], xs_ref.at[pl.ds(pos_ref[0, r], 1), :], sem))
    pltpu.make_async_copy(h_ref, xs_ref.at[pl.ds(0, tm), :], sem).wait()


def _dispatch(h2c, pos, n_rows):
    t = h2c.shape[0]
    tm = MOE_TOKEN_TILE
    return pl.pallas_call(
        _dispatch_kernel,
        grid=(t // tm,),
        in_specs=[
            pl.BlockSpec((1, tm), lambda i: (0, i), memory_space=pltpu.SMEM),
            pl.BlockSpec((tm, MOE_ROW), lambda i: (i, 0)),
            pl.BlockSpec(memory_space=pl.ANY),
        ],
        out_specs=pl.BlockSpec(memory_space=pl.ANY),
        out_shape=jax.ShapeDtypeStruct((n_rows, MOE_ROW), F32),
        scratch_shapes=[pltpu.SemaphoreType.DMA(())],
        input_output_aliases={2: 0},
        compiler_params=_params(("arbitrary",)),
        name="dispatch",
    )(pos, h2c, jnp.zeros((n_rows, MOE_ROW), F32))


def _expert_kernel(tile_group_ref, n_used_ref, xs_ref, wg_ref, wu_ref, wd_ref, y_ref):
    j = pl.program_id(0)

    @pl.when(j < n_used_ref[0])
    def _():
        h = xs_ref[:, :D_MODEL].astype(BF16)
        comb = xs_ref[:, D_MODEL:]
        lane = lax.broadcasted_iota(jnp.int32, comb.shape, 1)
        first = tile_group_ref[j] * EXPERTS_PER_GROUP
        y = None
        for e in range(EXPERTS_PER_GROUP):
            gate = jnp.dot(h, wg_ref[e], preferred_element_type=F32)
            up = jnp.dot(h, wu_ref[e], preferred_element_type=F32)
            w_e = jnp.sum(jnp.where(lane == first + e, comb, 0.0), axis=1, keepdims=True)
            act = (gate * _sigmoid(gate) * up * w_e).astype(BF16)
            part = jnp.dot(act, wd_ref[e], preferred_element_type=F32)
            y = part if y is None else y + part
        y_ref[...] = y

    @pl.when(j >= n_used_ref[0])
    def _():
        y_ref[...] = jnp.zeros(y_ref.shape, F32)


def _experts(xs, tile_group, n_used, wg, wu, wd):
    n_rows = xs.shape[0]
    tg = MOE_GROUP_TILE
    epg = EXPERTS_PER_GROUP
    return pl.pallas_call(
        _expert_kernel,
        grid_spec=pltpu.PrefetchScalarGridSpec(
            num_scalar_prefetch=2,
            grid=(n_rows // tg,),
            in_specs=[
                pl.BlockSpec((tg, MOE_ROW), lambda j, tgr, nu: (j, 0)),
                pl.BlockSpec((epg, D_MODEL, D_EXPERT), lambda j, tgr, nu: (tgr[j], 0, 0)),
                pl.BlockSpec((epg, D_MODEL, D_EXPERT), lambda j, tgr, nu: (tgr[j], 0, 0)),
                pl.BlockSpec((epg, D_EXPERT, D_MODEL), lambda j, tgr, nu: (tgr[j], 0, 0)),
            ],
            out_specs=pl.BlockSpec((tg, D_MODEL), lambda j, tgr, nu: (j, 0)),
        ),
        out_shape=jax.ShapeDtypeStruct((n_rows, D_MODEL), F32),
        compiler_params=_params(("arbitrary",)),
        name="experts",
    )(tile_group, n_used, xs, wg, wu, wd)


def _combine_kernel(pos_ref, x_ref, y_ref, fn_ref, out_ref, ybuf, sem, *, final_norm):
    tm = ybuf.shape[0]
    _start_row_copies(tm, lambda r: pltpu.make_async_copy(
        y_ref.at[pl.ds(pos_ref[0, r], 1), :], ybuf.at[pl.ds(r, 1), :], sem))
    pltpu.make_async_copy(y_ref.at[pl.ds(0, tm), :], ybuf, sem).wait()
    x = x_ref[...] + ybuf[...]
    if final_norm:
        ms = jnp.mean(x * x, axis=-1, keepdims=True)
        x = x * lax.rsqrt(ms + EPS) * fn_ref[...]
    out_ref[...] = x


def _combine(x_mid, y, pos, final_scale, final_norm):
    t = x_mid.shape[0]
    tm = MOE_TOKEN_TILE
    return pl.pallas_call(
        functools.partial(_combine_kernel, final_norm=final_norm),
        grid=(t // tm,),
        in_specs=[
            pl.BlockSpec((1, tm), lambda i: (0, i), memory_space=pltpu.SMEM),
            pl.BlockSpec((tm, D_MODEL), lambda i: (i, 0)),
            pl.BlockSpec(memory_space=pl.ANY),
            pl.BlockSpec((1, D_MODEL), lambda i: (0, 0)),
        ],
        out_specs=pl.BlockSpec((tm, D_MODEL), lambda i: (i, 0)),
        out_shape=jax.ShapeDtypeStruct((t, D_MODEL), F32),
        scratch_shapes=[pltpu.VMEM((tm, D_MODEL), F32), pltpu.SemaphoreType.DMA(())],
        compiler_params=_params(("arbitrary",)),
        name="combine",
    )(pos, x_mid, y, final_scale)


def _moe(x_mid, h2c, meta, counts, wg, wu, wd, final_scale, final_norm):
    t = x_mid.shape[0]
    tg = MOE_GROUP_TILE
    n_rows = t + N_GROUPS * tg
    cnt = counts[0, :N_GROUPS].astype(jnp.int32)
    padded = (cnt + tg - 1) // tg * tg
    ends = jnp.cumsum(padded)
    starts = ends - padded
    pos = (starts[meta[:, 0].astype(jnp.int32)] + meta[:, 1].astype(jnp.int32)).reshape(1, t)
    tile_start = jnp.arange(n_rows // tg, dtype=jnp.int32) * tg
    tile_group = jnp.minimum(jnp.sum(tile_start[:, None] >= ends[None, :], axis=1), N_GROUPS - 1)
    n_used = (ends[-1] // tg).reshape(1)
    xs = _dispatch(h2c, pos, n_rows)
    y = _experts(xs, tile_group.astype(jnp.int32), n_used.astype(jnp.int32), wg, wu, wd)
    return _combine(x_mid, y, pos, final_scale, final_norm)


def _rotary_tables(seq):
    inv = ROPE_THETA ** (-jnp.arange(0, HEAD_DIM, 2, dtype=F32) / HEAD_DIM)
    ang = jnp.arange(seq, dtype=F32)[:, None] * inv[None, :]
    cos, sin = jnp.cos(ang), jnp.sin(ang)
    return jnp.tile(cos, (1, 4)), jnp.concatenate([-sin, sin, -sin, sin], axis=1)


def _block_diag(w):
    nb = w.shape[0] // 4
    w4 = w.reshape(nb, 4, LRU_BLOCK, LRU_BLOCK)
    eye = jnp.eye(4, dtype=w.dtype)
    return jnp.einsum('qaij,ab->qaibj', w4, eye).reshape(nb, 4 * LRU_BLOCK, 4 * LRU_BLOCK)


def kernel(x, norm1_scale, w_in, b_merge, conv_w, conv_b, lru_wa, lru_ba, lru_wx, lru_bx, lru_lambda,
           w_proj_attn, w_proj_lru, w_out, norm2_scale, router_group_w, router_group_b, router_expert_w,
           router_expert_b, expert_w_gate, expert_w_up, expert_w_down, final_norm_scale):
    batch, seq, _ = x.shape
    depth = w_in.shape[0]
    cos_t, sin_t = _rotary_tables(seq)
    expand = (jnp.arange(LANES)[:, None] == (jnp.arange(GROUP_WIDTH)[None, :] // HEAD_DIM)).astype(BF16)
    expand = jnp.concatenate([expand, expand], axis=0)
    row = lambda v: v.reshape(1, -1)
    x2 = x.reshape(batch * seq, D_MODEL)
    for l in range(depth):
        lru_params = (conv_w[l], row(conv_b[l]), _block_diag(lru_wa[l]).astype(BF16),
                      _block_diag(lru_wx[l]).astype(BF16), row(lru_ba[l]), row(lru_bx[l]), row(lru_lambda[l]))
        *qkv, lru_out, gl = _in_proj(x2, row(norm1_scale[l]), w_in[l].astype(BF16), cos_t, sin_t, lru_params,
                                     batch, seq)
        o_list, lse_list = [], []
        for g in range(len(DILATIONS)):
            o, lse = _attention_group(qkv[g], qkv[3 + g], qkv[6 + g], g)
            o_list.append(o)
            lse_list.append(lse)
        rw = jnp.zeros((D_MODEL, LANES), F32)
        rw = rw.at[:, :N_EXPERTS].set(router_expert_w[l]).at[:, N_EXPERTS:N_EXPERTS + N_GROUPS].set(
            router_group_w[l])
        rb = jnp.zeros((1, LANES), F32)
        rb = rb.at[0, :N_EXPERTS].set(router_expert_b[l]).at[0, N_EXPERTS:N_EXPERTS + N_GROUPS].set(
            router_group_b[l])
        rw_hi = rw.astype(BF16)
        rw_split = jnp.concatenate([rw_hi, (rw - rw_hi.astype(F32)).astype(BF16)], axis=1)
        x_mid, h2c, meta, counts = _mix(x2, o_list, lse_list, lru_out, gl, row(b_merge[l]),
                                        w_proj_attn[l].astype(BF16), w_proj_lru[l].astype(BF16),
                                        w_out[l].astype(BF16), row(norm2_scale[l]), expand, rw_split, rb, seq)
        x2 = _moe(x_mid, h2c, meta, counts, expert_w_gate[l].astype(BF16), expert_w_up[l].astype(BF16),
                  expert_w_down[l].astype(BF16), row(final_norm_scale), final_norm=(l == depth - 1))
    return x2.reshape(batch, seq, D_MODEL)
```

```python
import functools

import jax
import jax.numpy as jnp
from jax import lax
from jax.experimental import pallas as pl
from jax.experimental.pallas import tpu as pltpu

D_MODEL = 1024
HEAD_DIM = 64
GROUP_WIDTH = 512
DILATIONS = (1, 4, 16)
WINDOW_STEPS = 128
ATTN_WIDTH = 3 * GROUP_WIDTH
LRU_WIDTH = 1024
LRU_BLOCK = 64
N_IN = 3 * ATTN_WIDTH + 2 * LRU_WIDTH + 2 * D_MODEL
CONV_WIDTH = 4
LRU_C = 8.0
N_GROUPS = 4
EXPERTS_PER_GROUP = 4
N_EXPERTS = 16
D_EXPERT = 512
EPS = 1e-6
ROPE_THETA = 10000.0
LANES = 128
VMEM_LIMIT = 56 * 1024 * 1024
IN_PROJ_TILE = 512
ATTN_QUERY_BLOCK = 128
MOE_ROW = D_MODEL + LANES
MOE_TOKEN_TILE = 2048
MOE_GROUP_TILE = 512

F32 = jnp.float32
BF16 = jnp.bfloat16


def _sigmoid(x):
    return 0.5 * jnp.tanh(0.5 * x) + 0.5


def _params(semantics):
    return pltpu.CompilerParams(dimension_semantics=semantics, vmem_limit_bytes=VMEM_LIMIT)


def _lru_gates(u, c, lru, ext, a_s, b_s):
    cw_ref, cb_ref, wa_ref, wx_ref, ba_ref, bx_ref, lam_ref = lru
    tm, width = u.shape
    ch = slice(c * width, (c + 1) * width)
    sub = 8
    ext[c, sub:, :] = u
    conv = cb_ref[:, ch] + cw_ref[0:1, ch] * ext[c, sub - 3:sub - 3 + tm, :]
    for j in range(1, CONV_WIDTH):
        conv = conv + cw_ref[j:j + 1, ch] * ext[c, sub - 3 + j:sub - 3 + j + tm, :]
    ext[c, 0:sub, :] = u[tm - sub:, :]
    xb = conv.astype(BF16)
    blk = 4 * LRU_BLOCK
    nb = width // blk
    ra = jnp.concatenate([jnp.dot(xb[:, q * blk:(q + 1) * blk], wa_ref[c * nb + q], preferred_element_type=F32)
                          for q in range(nb)], axis=1)
    rx = jnp.concatenate([jnp.dot(xb[:, q * blk:(q + 1) * blk], wx_ref[c * nb + q], preferred_element_type=F32)
                          for q in range(nb)], axis=1)
    r = _sigmoid(ra + ba_ref[:, ch])
    gate_i = _sigmoid(rx + bx_ref[:, ch])
    z = -lam_ref[:, ch]
    softplus = jnp.maximum(z, 0.0) + jnp.log1p(jnp.exp(-jnp.abs(z)))
    a = jnp.exp((-LRU_C) * r * softplus)
    a_s[c] = a
    one_m_a2 = 1.0 - a * a
    root = jnp.where(one_m_a2 > 0.0, one_m_a2 * lax.rsqrt(one_m_a2), 0.0)
    b_s[c] = root * (gate_i * conv)


def _lru_scan(c, a_s, b_s, carry):
    sub = 8
    tm, width = a_s.shape[1:]
    row = lax.broadcasted_iota(jnp.int32, (sub, width), 0)
    h_prev = carry[c]
    for k in range(tm // sub):
        rows = slice(k * sub, (k + 1) * sub)
        a8 = a_s[c, rows, :]
        b8 = b_s[c, rows, :]
        for sh in (1, 2, 4):
            keep = row >= sh
            a_sh = jnp.where(keep, pltpu.roll(a8, sh, 0), 1.0)
            b_sh = jnp.where(keep, pltpu.roll(b8, sh, 0), 0.0)
            b8 = a8 * b_sh + b8
            a8 = a8 * a_sh
        h8 = b8 + a8 * h_prev
        b_s[c, rows, :] = h8
        h_prev = jnp.broadcast_to(h8[sub - 1:sub, :], (sub, width))
    carry[c] = h_prev


def _in_proj_kernel(x_ref, scale_ref, w_ref, cos_ref, sin_ref, *rest, n_seq_tiles):
    lru, qkv_refs = rest[:7], rest[7:16]
    lru_ref, gl_ref, h32_ref, hn_ref, ext, a_s, b_s, carry = rest[16:]
    half = LRU_WIDTH // 2

    @pl.when(pl.program_id(0) % n_seq_tiles == 0)
    def _():
        ext[:, 0:8, :] = jnp.zeros((2, 8, half), F32)
        carry[...] = jnp.zeros(carry.shape, F32)

    x = x_ref[...]
    tm = x.shape[0]
    ms = jnp.mean(x * x, axis=-1, keepdims=True)
    hn = x * lax.rsqrt(ms + EPS) * scale_ref[...]
    hn_ref[0] = hn.astype(BF16)
    n_lane_blocks = D_MODEL // LANES
    for j in range(n_lane_blocks):
        h32_ref[j] = hn[:, j * LANES:(j + 1) * LANES]
    for g in (1, 2):
        d = DILATIONS[g]
        rows = tm // d
        for r in range(d):
            for j in range(n_lane_blocks):
                hn_ref[g, r * rows:(r + 1) * rows, j * LANES:(j + 1) * LANES] = (
                    h32_ref[j, pl.ds(r, rows, stride=d), :].astype(BF16))
    lane = lax.broadcasted_iota(jnp.int32, (tm, LANES), 1)
    first_half = (lane % HEAD_DIM) < (HEAD_DIM // 2)
    n_chunks = N_IN // GROUP_WIDTH
    for c in (9, 10) + tuple(range(9)) + tuple(range(13, n_chunks)) + (11, 12):
        g = c % 3 if c < 9 else 0
        res = jnp.dot(hn_ref[g], w_ref[:, c * GROUP_WIDTH:(c + 1) * GROUP_WIDTH],
                      preferred_element_type=F32)
        if c < 6:
            d = DILATIONS[g]
            cos, sin = (jnp.concatenate([t_ref[pl.ds(r, tm // d, stride=d), :] for r in range(d)], axis=0)
                        if d > 1 else t_ref[...] for t_ref in (cos_ref, sin_ref))
            parts = []
            for s in range(GROUP_WIDTH // LANES):
                blk = res[:, s * LANES:(s + 1) * LANES]
                partner = jnp.where(first_half, pltpu.roll(blk, LANES - HEAD_DIM // 2, 1),
                                    pltpu.roll(blk, HEAD_DIM // 2, 1))
                o = blk * cos + partner * sin
                if c < 3:
                    o = o * (HEAD_DIM ** -0.5)
                parts.append(o)
            res = jnp.concatenate(parts, axis=1)
        if c < 9:
            d = DILATIONS[g]
            qkv_refs[c][...] = res.reshape(d, tm // d, GROUP_WIDTH).astype(BF16)
        elif c < 11:
            _lru_gates(res, c - 9, lru, ext, a_s, b_s)
            _lru_scan(c - 9, a_s, b_s, carry)
        elif c < 13:
            hc = c - 11
            lru_ref[:, hc * half:(hc + 1) * half] = (_gelu_tanh(res) * b_s[hc]).astype(BF16)
        else:
            gl_ref[:, (c - 13) * GROUP_WIDTH:(c - 12) * GROUP_WIDTH] = res.astype(BF16)


def _in_proj(x2, scale, w_bf, cos_t, sin_t, lru_params, batch, seq):
    t = x2.shape[0]
    tm = IN_PROJ_TILE
    n_seq_tiles = seq // tm
    half = LRU_WIDTH // 2
    full = lambda a: pl.BlockSpec(a.shape, lambda i: (0,) * a.ndim)
    qkv_specs, qkv_shapes = [], []
    for c in range(9):
        d = DILATIONS[c % 3]
        qkv_specs.append(pl.BlockSpec((None, d, tm // d, GROUP_WIDTH),
                                      lambda i: (i // n_seq_tiles, 0, i % n_seq_tiles, 0)))
        qkv_shapes.append(jax.ShapeDtypeStruct((batch, d, seq // d, GROUP_WIDTH), BF16))
    return pl.pallas_call(
        functools.partial(_in_proj_kernel, n_seq_tiles=n_seq_tiles),
        grid=(t // tm,),
        in_specs=[
            pl.BlockSpec((tm, D_MODEL), lambda i: (i, 0)),
            pl.BlockSpec((1, D_MODEL), lambda i: (0, 0)),
            pl.BlockSpec((D_MODEL, N_IN), lambda i: (0, 0), pipeline_mode=pl.Buffered(1)),
            pl.BlockSpec((tm, LANES), lambda i: (i % n_seq_tiles, 0)),
            pl.BlockSpec((tm, LANES), lambda i: (i % n_seq_tiles, 0)),
        ] + [full(a) for a in lru_params],
        out_specs=qkv_specs + [
            pl.BlockSpec((tm, LRU_WIDTH), lambda i: (i, 0)),
            pl.BlockSpec((tm, 2 * D_MODEL), lambda i: (i, 0)),
        ],
        out_shape=qkv_shapes + [
            jax.ShapeDtypeStruct((t, LRU_WIDTH), BF16),
            jax.ShapeDtypeStruct((t, 2 * D_MODEL), BF16),
        ],
        scratch_shapes=[
            pltpu.VMEM((D_MODEL // LANES, tm, LANES), F32),
            pltpu.VMEM((3, tm, D_MODEL), BF16),
            pltpu.VMEM((2, tm + 8, half), F32),
            pltpu.VMEM((2, tm, half), F32),
            pltpu.VMEM((2, tm, half), F32),
            pltpu.VMEM((2, 8, half), F32),
        ],
        compiler_params=_params(("arbitrary",)),
        name="in_proj",
    )(x2, scale, w_bf, cos_t, sin_t, *lru_params)


def _attn_kernel(q_ref, kp_ref, kc_ref, vp_ref, vc_ref, o_ref, lse_ref, kcat, vcat, *, tq):
    w = WINDOW_STEPS
    qb = ATTN_QUERY_BLOCK
    kcat[0:w, :] = kp_ref[...]
    kcat[w:w + tq, :] = kc_ref[...]
    kcat[w + tq:, :] = jnp.zeros((qb, GROUP_WIDTH), BF16)
    vcat[0:w, :] = vp_ref[...]
    vcat[w:w + tq, :] = vc_ref[...]
    vcat[w + tq:, :] = jnp.zeros((qb, GROUP_WIDTH), BF16)
    first_tile = pl.program_id(2) == 0

    qi = lax.broadcasted_iota(jnp.int32, (2 * qb, 2 * w), 0) % qb
    col = lax.broadcasted_iota(jnp.int32, (2 * qb, 2 * w), 1)
    band = (col >= qi) & (col <= qi + w)
    lane_q = lax.broadcasted_iota(jnp.int32, (qb, LANES), 1)
    low_q = lane_q < HEAD_DIM

    n_pairs = GROUP_WIDTH // LANES

    def query_block(s, carry):
        r0 = s * qb
        min_col = jnp.where(first_tile, w - r0, 0)
        valid = band & (col >= min_col)
        lse_tile = jnp.zeros((qb, LANES), F32)
        for p in range(n_pairs):
            lanes = slice(p * LANES, (p + 1) * LANES)
            qp = q_ref[pl.ds(r0, qb), lanes]
            zero = jnp.zeros_like(qp)
            qs = jnp.concatenate([jnp.where(low_q, qp, zero), jnp.where(low_q, zero, qp)], axis=0)
            kk = kcat[pl.ds(r0, 2 * w), lanes]
            vv = vcat[pl.ds(r0, 2 * w), lanes]
            sc = lax.dot_general(qs, kk, (((1,), (1,)), ((), ())), preferred_element_type=F32)
            sc = jnp.where(valid, sc, -jnp.inf)
            m = jnp.max(sc, axis=1, keepdims=True)
            pr = jnp.exp(sc - m)
            den = jnp.sum(pr, axis=1, keepdims=True)
            pv = jnp.dot(pr.astype(BF16), vv, preferred_element_type=F32) / den
            o_ref[pl.ds(r0, qb), lanes] = jnp.where(low_q, pv[:qb], pv[qb:]).astype(BF16)
            lse = m + jnp.log(den)
            lse_tile = jnp.where(lane_q == 2 * p, lse[:qb], lse_tile)
            lse_tile = jnp.where(lane_q == 2 * p + 1, lse[qb:], lse_tile)
        lse_ref[pl.ds(r0, qb), :] = lse_tile
        return carry

    for s in range(tq // qb):
        query_block(s, 0)


def _attention_group(q, k, v, group):
    batch, d, length, _ = q.shape
    tq = 1024
    w = WINDOW_STEPS
    sub = tq // w
    cur = pl.BlockSpec((None, None, tq, GROUP_WIDTH), lambda b, r, i: (b, r, i, 0))
    prev = pl.BlockSpec((None, None, w, GROUP_WIDTH),
                        lambda b, r, i: (b, r, jnp.maximum(i * sub - 1, 0), 0))
    return pl.pallas_call(
        functools.partial(_attn_kernel, tq=tq),
        grid=(batch, d, length // tq),
        in_specs=[cur, prev, cur, prev, cur],
        out_specs=[
            pl.BlockSpec((None, None, tq, GROUP_WIDTH), lambda b, r, i: (b, r, i, 0)),
            pl.BlockSpec((None, None, tq, LANES), lambda b, r, i: (b, r, i, 0)),
        ],
        out_shape=[
            jax.ShapeDtypeStruct((batch, d, length, GROUP_WIDTH), BF16),
            jax.ShapeDtypeStruct((batch, d, length, LANES), F32),
        ],
        scratch_shapes=[pltpu.VMEM((tq + w + ATTN_QUERY_BLOCK, GROUP_WIDTH), BF16)] * 2,
        compiler_params=_params(("arbitrary", "arbitrary", "arbitrary")),
        name=f"attn_d{d}",
    )(q, k, k, v, v)


def _gelu_tanh(x):
    return 0.5 * x * (1.0 + jnp.tanh(0.7978845608028654 * (x + 0.044715 * (x * x * x))))


def _split_bf16(x):
    hi = x.astype(BF16)
    lo = (x - hi.astype(F32)).astype(BF16)
    return hi, lo


def _mix_kernel(x_ref, o1_ref, o2_ref, o3_ref, l1_ref, l2_ref, l3_ref, lru_ref, gl_ref, bm_ref,
                wpa_ref, wpl_ref, wo_ref, n2_ref, expand_ref, rw_ref, rb_ref, tri_ref,
                xo_ref, h2c_ref, meta_ref, cnt_ref, o2_s, o3_s, l2_s, l3_s, cnt_s):
    tm = x_ref.shape[0]

    @pl.when(pl.program_id(0) == 0)
    def _():
        cnt_s[...] = jnp.zeros(cnt_s.shape, F32)

    for src_o, src_l, dst_o, dst_l in ((o2_ref, l2_ref, o2_s, l2_s), (o3_ref, l3_ref, o3_s, l3_s)):
        d = src_o.shape[0]
        for r in range(d):
            o_r = src_o[r].astype(F32)
            for j in range(GROUP_WIDTH // LANES):
                dst_o[j, pl.ds(r, tm // d, stride=d), :] = o_r[:, j * LANES:(j + 1) * LANES]
            dst_l[pl.ds(r, tm // d, stride=d), :] = src_l[r]
    tok = lambda s: jnp.concatenate([s[j] for j in range(GROUP_WIDTH // LANES)], axis=1)
    l1, l2, l3 = l1_ref[0], l2_s[...], l3_s[...]
    mx = jnp.maximum(jnp.maximum(l1, l2), l3)
    e1, e2, e3 = jnp.exp(l1 - mx), jnp.exp(l2 - mx), jnp.exp(l3 - mx)
    inv = 1.0 / (e1 + e2 + e3)
    attn = None
    for e, o_tok in ((e1, o1_ref[0].astype(F32)), (e2, tok(o2_s)), (e3, tok(o3_s))):
        hi, lo = _split_bf16(e * inv)
        wide = jnp.dot(jnp.concatenate([hi, lo], axis=1), expand_ref[...], preferred_element_type=F32)
        term = wide * o_tok
        attn = term if attn is None else attn + term
    pa = jnp.dot(attn.astype(BF16), wpa_ref[...], preferred_element_type=F32)
    pr = jnp.dot(lru_ref[...], wpl_ref[...], preferred_element_type=F32)
    gates = _sigmoid(gl_ref[...].astype(F32) + bm_ref[...])
    mixed = gates[:, :D_MODEL] * pa + gates[:, D_MODEL:] * pr
    x = x_ref[...] + jnp.dot(mixed.astype(BF16), wo_ref[...], preferred_element_type=F32)
    xo_ref[...] = x

    ms = jnp.mean(x * x, axis=-1, keepdims=True)
    h2 = x * lax.rsqrt(ms + EPS) * n2_ref[...]
    h_hi, h_lo = _split_bf16(h2)
    both = jnp.dot(h_hi, rw_ref[...], preferred_element_type=F32)
    logits = (both[:, :LANES] + both[:, LANES:]
              + jnp.dot(h_lo, rw_ref[:, :LANES], preferred_element_type=F32)) + rb_ref[...]
    comb, g_idx = _route(logits)
    h2c_ref[:, :D_MODEL] = h2
    h2c_ref[:, D_MODEL:] = comb

    lane = lax.broadcasted_iota(jnp.int32, (tm, LANES), 1)
    onehot = lane == g_idx
    before = jnp.dot(tri_ref[...], onehot.astype(BF16), preferred_element_type=F32) + cnt_s[0:1, :]
    rank = jnp.sum(jnp.where(onehot, before, 0.0), axis=1, keepdims=True)
    meta_ref[...] = jnp.where(lane == 0, g_idx.astype(F32), jnp.where(lane == 1, rank, 0.0))
    cnt_s[...] = cnt_s[...] + jnp.sum(onehot.astype(F32), axis=0, keepdims=True)
    cnt_ref[...] = cnt_s[...]


def _route(logits):
    tm = logits.shape[0]
    lane = lax.broadcasted_iota(jnp.int32, (tm, LANES), 1)
    neg = -jnp.inf
    big = jnp.int32(LANES)
    is_g = (lane >= N_EXPERTS) & (lane < N_EXPERTS + N_GROUPS)
    gl = jnp.where(is_g, logits, neg)
    gmax = jnp.max(gl, axis=1, keepdims=True)
    g_idx = jnp.min(jnp.where(gl == gmax, lane, big), axis=1, keepdims=True) - N_EXPERTS
    g_w = 1.0 / jnp.sum(jnp.exp(gl - gmax), axis=1, keepdims=True)
    lo = g_idx * EXPERTS_PER_GROUP
    in_grp = (lane >= lo) & (lane < lo + EXPERTS_PER_GROUP)
    el = jnp.where(in_grp, logits, neg)
    v1 = jnp.max(el, axis=1, keepdims=True)
    i1 = jnp.min(jnp.where(el == v1, lane, big), axis=1, keepdims=True)
    el2 = jnp.where(lane == i1, neg, el)
    v2 = jnp.max(el2, axis=1, keepdims=True)
    i2 = jnp.min(jnp.where(el2 == v2, lane, big), axis=1, keepdims=True)
    e2 = jnp.exp(v2 - v1)
    w1 = g_w / (1.0 + e2)
    w2 = g_w * e2 / (1.0 + e2)
    return jnp.where(lane == i1, w1, 0.0) + jnp.where(lane == i2, w2, 0.0), g_idx


def _mix(x2, o_list, lse_list, lru_out, gl, b_merge, wpa, wpl, wo, norm2, expand, rw_split, rb, seq):
    t = x2.shape[0]
    tm = IN_PROJ_TILE
    n_seq_tiles = seq // tm
    row = lambda width: pl.BlockSpec((tm, width), lambda i: (i, 0))
    full = lambda a: pl.BlockSpec(a.shape, lambda i: (0,) * a.ndim)
    resid = lambda d, width: pl.BlockSpec((None, d, tm // d, width),
                                          lambda i: (i // n_seq_tiles, 0, i % n_seq_tiles, 0))
    tri = (jnp.arange(tm)[:, None] > jnp.arange(tm)[None, :]).astype(BF16)
    consts = [b_merge, wpa, wpl, wo, norm2, expand, rw_split, rb, tri]
    return pl.pallas_call(
        _mix_kernel,
        grid=(t // tm,),
        in_specs=[row(D_MODEL)] + [resid(d, GROUP_WIDTH) for d in DILATIONS]
                 + [resid(d, LANES) for d in DILATIONS]
                 + [row(LRU_WIDTH), row(2 * D_MODEL)] + [full(a) for a in consts],
        out_specs=[row(D_MODEL), row(MOE_ROW), row(LANES), pl.BlockSpec((8, LANES), lambda i: (0, 0))],
        out_shape=[
            jax.ShapeDtypeStruct((t, D_MODEL), F32),
            jax.ShapeDtypeStruct((t, MOE_ROW), F32),
            jax.ShapeDtypeStruct((t, LANES), F32),
            jax.ShapeDtypeStruct((8, LANES), F32),
        ],
        scratch_shapes=[pltpu.VMEM((GROUP_WIDTH // LANES, tm, LANES), F32),
                        pltpu.VMEM((GROUP_WIDTH // LANES, tm, LANES), F32),
                        pltpu.VMEM((tm, LANES), F32), pltpu.VMEM((tm, LANES), F32),
                        pltpu.VMEM((8, LANES), F32)],
        compiler_params=_params(("arbitrary",)),
        name="mix",
    )(x2, *o_list, *lse_list, lru_out, gl, *consts)


def _start_row_copies(n_rows, make_copy):
    for r in range(n_rows):
        make_copy(r).start(priority=r % 2)


def _dispatch_kernel(fill_ref, pos_ref, h_ref, xs_ref, zero_row, sem, fill_sem):
    tm = h_ref.shape[0]
    _start_row_copies(tm, lambda r: pltpu.make_async_copy(
        h_ref.at[pl.ds(r, 1), :], xs_ref.at[pl.ds(pos_ref[0, r], 1), :], sem))
    pltpu.make_async_copy(h_ref, xs_ref.at[pl.ds(0, tm), :], sem).wait()

    @pl.when(pl.program_id(0) == pl.num_programs(0) - 1)
    def _():
        zero_row[...] = jnp.zeros(zero_row.shape, F32)
        for g in range(N_GROUPS + 1):
            first, count = fill_ref[2 * g], fill_ref[2 * g + 1]
            pad_copy = lambda r: pltpu.make_async_copy(
                zero_row.at[pl.ds(0, 1), :], xs_ref.at[pl.ds(first + r, 1), :], fill_sem)

            def start(r, c):
                pad_copy(r).start()
                return c

            def wait(r, c):
                pad_copy(r).wait()
                return c

            lax.fori_loop(0, count, start, 0)
            lax.fori_loop(0, count, wait, 0)


def _dispatch(h2c, pos, fill, n_rows):
    t = h2c.shape[0]
    tm = MOE_TOKEN_TILE
    return pl.pallas_call(
        _dispatch_kernel,
        grid_spec=pltpu.PrefetchScalarGridSpec(
            num_scalar_prefetch=1,
            grid=(t // tm,),
            in_specs=[
                pl.BlockSpec((1, tm), lambda i, fill: (0, i), memory_space=pltpu.SMEM),
                pl.BlockSpec((tm, MOE_ROW), lambda i, fill: (i, 0)),
            ],
            out_specs=pl.BlockSpec(memory_space=pl.ANY),
            scratch_shapes=[pltpu.VMEM((8, MOE_ROW), F32), pltpu.SemaphoreType.DMA(()),
                            pltpu.SemaphoreType.DMA(())],
        ),
        out_shape=jax.ShapeDtypeStruct((n_rows, MOE_ROW), F32),
        compiler_params=_params(("arbitrary",)),
        name="dispatch",
    )(fill, pos, h2c)


def _expert_kernel(tile_group_ref, n_used_ref, xs_ref, wg_ref, wu_ref, wd_ref, y_ref):
    j = pl.program_id(0)

    @pl.when(j < n_used_ref[0])
    def _():
        h = xs_ref[:, :D_MODEL].astype(BF16)
        comb = xs_ref[:, D_MODEL:]
        lane = lax.broadcasted_iota(jnp.int32, comb.shape, 1)
        first = tile_group_ref[j] * EXPERTS_PER_GROUP
        y = None
        for e in range(EXPERTS_PER_GROUP):
            gate = jnp.dot(h, wg_ref[e], preferred_element_type=F32)
            up = jnp.dot(h, wu_ref[e], preferred_element_type=F32)
            w_e = jnp.sum(jnp.where(lane == first + e, comb, 0.0), axis=1, keepdims=True)
            act = (gate * _sigmoid(gate) * up * w_e).astype(BF16)
            part = jnp.dot(act, wd_ref[e], preferred_element_type=F32)
            y = part if y is None else y + part
        y_ref[...] = y

    @pl.when(j >= n_used_ref[0])
    def _():
        y_ref[...] = jnp.zeros(y_ref.shape, F32)


def _experts(xs, tile_group, n_used, wg, wu, wd):
    n_rows = xs.shape[0]
    tg = MOE_GROUP_TILE
    epg = EXPERTS_PER_GROUP
    return pl.pallas_call(
        _expert_kernel,
        grid_spec=pltpu.PrefetchScalarGridSpec(
            num_scalar_prefetch=2,
            grid=(n_rows // tg,),
            in_specs=[
                pl.BlockSpec((tg, MOE_ROW), lambda j, tgr, nu: (jnp.minimum(j, nu[0] - 1), 0)),
                pl.BlockSpec((epg, D_MODEL, D_EXPERT), lambda j, tgr, nu: (tgr[j], 0, 0)),
                pl.BlockSpec((epg, D_MODEL, D_EXPERT), lambda j, tgr, nu: (tgr[j], 0, 0)),
                pl.BlockSpec((epg, D_EXPERT, D_MODEL), lambda j, tgr, nu: (tgr[j], 0, 0)),
            ],
            out_specs=pl.BlockSpec((tg, D_MODEL), lambda j, tgr, nu: (j, 0)),
        ),
        out_shape=jax.ShapeDtypeStruct((n_rows, D_MODEL), F32),
        compiler_params=_params(("arbitrary",)),
        name="experts",
    )(tile_group, n_used, xs, wg, wu, wd)


def _combine_kernel(pos_ref, x_ref, y_ref, fn_ref, out_ref, ybuf, sem, *, final_norm):
    tm = ybuf.shape[0]
    _start_row_copies(tm, lambda r: pltpu.make_async_copy(
        y_ref.at[pl.ds(pos_ref[0, r], 1), :], ybuf.at[pl.ds(r, 1), :], sem))
    pltpu.make_async_copy(y_ref.at[pl.ds(0, tm), :], ybuf, sem).wait()
    x = x_ref[...] + ybuf[...]
    if final_norm:
        ms = jnp.mean(x * x, axis=-1, keepdims=True)
        x = x * lax.rsqrt(ms + EPS) * fn_ref[...]
    out_ref[...] = x


def _combine(x_mid, y, pos, final_scale, final_norm):
    t = x_mid.shape[0]
    tm = MOE_TOKEN_TILE
    return pl.pallas_call(
        functools.partial(_combine_kernel, final_norm=final_norm),
        grid=(t // tm,),
        in_specs=[
            pl.BlockSpec((1, tm), lambda i: (0, i), memory_space=pltpu.SMEM),
            pl.BlockSpec((tm, D_MODEL), lambda i: (i, 0)),
            pl.BlockSpec(memory_space=pl.ANY),
            pl.BlockSpec((1, D_MODEL), lambda i: (0, 0)),
        ],
        out_specs=pl.BlockSpec((tm, D_MODEL), lambda i: (i, 0)),
        out_shape=jax.ShapeDtypeStruct((t, D_MODEL), F32),
        scratch_shapes=[pltpu.VMEM((tm, D_MODEL), F32), pltpu.SemaphoreType.DMA(())],
        compiler_params=_params(("arbitrary",)),
        name="combine",
    )(pos, x_mid, y, final_scale)


def _moe(x_mid, h2c, meta, counts, wg, wu, wd, final_scale, final_norm):
    t = x_mid.shape[0]
    tg = MOE_GROUP_TILE
    n_rows = t + N_GROUPS * tg
    cnt = counts[0, :N_GROUPS].astype(jnp.int32)
    padded = (cnt + tg - 1) // tg * tg
    ends = jnp.cumsum(padded)
    starts = ends - padded
    pos = (starts[meta[:, 0].astype(jnp.int32)] + meta[:, 1].astype(jnp.int32)).reshape(1, t)
    tile_start = jnp.arange(n_rows // tg, dtype=jnp.int32) * tg
    tile_group = jnp.minimum(jnp.sum(tile_start[:, None] >= ends[None, :], axis=1), N_GROUPS - 1)
    n_used = (ends[-1] // tg).reshape(1)
    fill = jnp.stack([jnp.append(starts + cnt, ends[-1]), jnp.append(padded - cnt, n_rows - ends[-1])],
                     axis=1).reshape(-1)
    xs = _dispatch(h2c, pos, fill.astype(jnp.int32), n_rows)
    y = _experts(xs, tile_group.astype(jnp.int32), n_used.astype(jnp.int32), wg, wu, wd)
    return _combine(x_mid, y, pos, final_scale, final_norm)


def _rotary_tables(seq):
    inv = ROPE_THETA ** (-jnp.arange(0, HEAD_DIM, 2, dtype=F32) / HEAD_DIM)
    ang = jnp.arange(seq, dtype=F32)[:, None] * inv[None, :]
    cos, sin = jnp.cos(ang), jnp.sin(ang)
    return jnp.tile(cos, (1, 4)), jnp.concatenate([-sin, sin, -sin, sin], axis=1)


def _block_diag(w):
    nb = w.shape[0] // 4
    w4 = w.reshape(nb, 4, LRU_BLOCK, LRU_BLOCK)
    eye = jnp.eye(4, dtype=w.dtype)
    return jnp.einsum('qaij,ab->qaibj', w4, eye).reshape(nb, 4 * LRU_BLOCK, 4 * LRU_BLOCK)


def kernel(x, norm1_scale, w_in, b_merge, conv_w, conv_b, lru_wa, lru_ba, lru_wx, lru_bx, lru_lambda,
           w_proj_attn, w_proj_lru, w_out, norm2_scale, router_group_w, router_group_b, router_expert_w,
           router_expert_b, expert_w_gate, expert_w_up, expert_w_down, final_norm_scale):
    batch, seq, _ = x.shape
    depth = w_in.shape[0]
    cos_t, sin_t = _rotary_tables(seq)
    expand = (jnp.arange(LANES)[:, None] == (jnp.arange(GROUP_WIDTH)[None, :] // HEAD_DIM)).astype(BF16)
    expand = jnp.concatenate([expand, expand], axis=0)
    row = lambda v: v.reshape(1, -1)
    x2 = x.reshape(batch * seq, D_MODEL)
    for l in range(depth):
        lru_params = (conv_w[l], row(conv_b[l]), _block_diag(lru_wa[l]).astype(BF16),
                      _block_diag(lru_wx[l]).astype(BF16), row(lru_ba[l]), row(lru_bx[l]), row(lru_lambda[l]))
        *qkv, lru_out, gl = _in_proj(x2, row(norm1_scale[l]), w_in[l].astype(BF16), cos_t, sin_t, lru_params,
                                     batch, seq)
        o_list, lse_list = [], []
        for g in range(len(DILATIONS)):
            o, lse = _attention_group(qkv[g], qkv[3 + g], qkv[6 + g], g)
            o_list.append(o)
            lse_list.append(lse)
        rw = jnp.zeros((D_MODEL, LANES), F32)
        rw = rw.at[:, :N_EXPERTS].set(router_expert_w[l]).at[:, N_EXPERTS:N_EXPERTS + N_GROUPS].set(
            router_group_w[l])
        rb = jnp.zeros((1, LANES), F32)
        rb = rb.at[0, :N_EXPERTS].set(router_expert_b[l]).at[0, N_EXPERTS:N_EXPERTS + N_GROUPS].set(
            router_group_b[l])
        rw_hi = rw.astype(BF16)
        rw_split = jnp.concatenate([rw_hi, (rw - rw_hi.astype(F32)).astype(BF16)], axis=1)
        x_mid, h2c, meta, counts = _mix(x2, o_list, lse_list, lru_out, gl, row(b_merge[l]),
                                        w_proj_attn[l].astype(BF16), w_proj_lru[l].astype(BF16),
                                        w_out[l].astype(BF16), row(norm2_scale[l]), expand, rw_split, rb, seq)
        x2 = _moe(x_mid, h2c, meta, counts, expert_w_gate[l].astype(BF16), expert_w_up[l].astype(BF16),
                  expert_w_down[l].astype(BF16), row(final_norm_scale), final_norm=(l == depth - 1))
    return x2.reshape(batch, seq, D_MODEL)
```

```python
import functools

import jax
import jax.numpy as jnp
from jax import lax
from jax.experimental import pallas as pl
from jax.experimental.pallas import tpu as pltpu

D_MODEL = 1024
HEAD_DIM = 64
GROUP_WIDTH = 512
DILATIONS = (1, 4, 16)
WINDOW_STEPS = 128
ATTN_WIDTH = 3 * GROUP_WIDTH
LRU_WIDTH = 1024
LRU_BLOCK = 64
N_IN = 3 * ATTN_WIDTH + 2 * LRU_WIDTH + 2 * D_MODEL
CONV_WIDTH = 4
LRU_C = 8.0
N_GROUPS = 4
EXPERTS_PER_GROUP = 4
N_EXPERTS = 16
D_EXPERT = 512
EPS = 1e-6
ROPE_THETA = 10000.0
LANES = 128
SUBLANES = 8
VMEM_LIMIT = 56 * 1024 * 1024
IN_PROJ_TILE = 512
ATTN_QUERY_BLOCK = 128
ATTN_TILE = 2048
MOE_ROW = D_MODEL + LANES
MOE_TOKEN_TILE = 2048
MOE_GROUP_TILE = 512

F32 = jnp.float32
BF16 = jnp.bfloat16


def _sigmoid(x):
    return 0.5 * jnp.tanh(0.5 * x) + 0.5


def _params(semantics):
    return pltpu.CompilerParams(dimension_semantics=semantics, vmem_limit_bytes=VMEM_LIMIT)


def _lru_gates(u, c, lru, ext, a_s, b_s):
    cw_ref, cb_ref, wa_ref, wx_ref, ba_ref, bx_ref, lam_ref = lru
    tm, width = u.shape
    ch = slice(c * width, (c + 1) * width)
    sub = SUBLANES
    ext[c, sub:, :] = u
    conv = cb_ref[:, ch] + cw_ref[0:1, ch] * ext[c, sub - 3:sub - 3 + tm, :]
    for j in range(1, CONV_WIDTH):
        conv = conv + cw_ref[j:j + 1, ch] * ext[c, sub - 3 + j:sub - 3 + j + tm, :]
    ext[c, 0:sub, :] = u[tm - sub:, :]
    xb = conv.astype(BF16)
    blk = 4 * LRU_BLOCK
    nb = width // blk
    ra = jnp.concatenate([jnp.dot(xb[:, q * blk:(q + 1) * blk], wa_ref[c * nb + q], preferred_element_type=F32)
                          for q in range(nb)], axis=1)
    rx = jnp.concatenate([jnp.dot(xb[:, q * blk:(q + 1) * blk], wx_ref[c * nb + q], preferred_element_type=F32)
                          for q in range(nb)], axis=1)
    r = _sigmoid(ra + ba_ref[:, ch])
    gate_i = _sigmoid(rx + bx_ref[:, ch])
    z = -lam_ref[:, ch]
    softplus = jnp.maximum(z, 0.0) + jnp.log1p(jnp.exp(-jnp.abs(z)))
    a = jnp.exp((-LRU_C) * r * softplus)
    a_s[c] = a
    one_m_a2 = 1.0 - a * a
    root = jnp.where(one_m_a2 > 0.0, one_m_a2 * lax.rsqrt(one_m_a2), 0.0)
    b_s[c] = root * (gate_i * conv)


def _lru_scan(c, a_s, b_s, carry):
    sub = SUBLANES
    tm, width = a_s.shape[1:]
    row = lax.broadcasted_iota(jnp.int32, (sub, width), 0)
    h_prev = carry[c]
    for k in range(tm // sub):
        rows = slice(k * sub, (k + 1) * sub)
        a8 = a_s[c, rows, :]
        b8 = b_s[c, rows, :]
        for sh in (1, 2, 4):
            keep = row >= sh
            a_sh = jnp.where(keep, pltpu.roll(a8, sh, 0), 1.0)
            b_sh = jnp.where(keep, pltpu.roll(b8, sh, 0), 0.0)
            b8 = a8 * b_sh + b8
            a8 = a8 * a_sh
        h8 = b8 + a8 * h_prev
        b_s[c, rows, :] = h8
        h_prev = jnp.broadcast_to(h8[sub - 1:sub, :], (sub, width))
    carry[c] = h_prev


def _in_proj_kernel(x_ref, scale_ref, w_ref, cos_ref, sin_ref, *rest, n_seq_tiles):
    lru, qkv_refs = rest[:7], rest[7:16]
    lru_ref, gl_ref, h32_ref, hn_ref, ext, a_s, b_s, carry = rest[16:]
    half = LRU_WIDTH // 2

    @pl.when(pl.program_id(0) % n_seq_tiles == 0)
    def _():
        ext[:, 0:SUBLANES, :] = jnp.zeros((2, SUBLANES, half), F32)
        carry[...] = jnp.zeros(carry.shape, F32)

    x = x_ref[...]
    tm = x.shape[0]
    ms = jnp.mean(x * x, axis=-1, keepdims=True)
    hn = x * lax.rsqrt(ms + EPS) * scale_ref[...]
    hn_ref[0] = hn.astype(BF16)
    n_lane_blocks = D_MODEL // LANES
    for j in range(n_lane_blocks):
        h32_ref[j] = hn[:, j * LANES:(j + 1) * LANES]
    for g in (1, 2):
        d = DILATIONS[g]
        rows = tm // d
        for r in range(d):
            for j in range(n_lane_blocks):
                hn_ref[g, r * rows:(r + 1) * rows, j * LANES:(j + 1) * LANES] = (
                    h32_ref[j, pl.ds(r, rows, stride=d), :].astype(BF16))
    lane = lax.broadcasted_iota(jnp.int32, (tm, LANES), 1)
    first_half = (lane % HEAD_DIM) < (HEAD_DIM // 2)
    n_chunks = N_IN // GROUP_WIDTH
    for c in (9, 10) + tuple(range(9)) + tuple(range(13, n_chunks)) + (11, 12):
        g = c % 3 if c < 9 else 0
        res = jnp.dot(hn_ref[g], w_ref[:, c * GROUP_WIDTH:(c + 1) * GROUP_WIDTH],
                      preferred_element_type=F32)
        if c < 6:
            d = DILATIONS[g]
            cos, sin = (jnp.concatenate([t_ref[pl.ds(r, tm // d, stride=d), :] for r in range(d)], axis=0)
                        if d > 1 else t_ref[...] for t_ref in (cos_ref, sin_ref))
            parts = []
            for s in range(GROUP_WIDTH // LANES):
                blk = res[:, s * LANES:(s + 1) * LANES]
                partner = jnp.where(first_half, pltpu.roll(blk, LANES - HEAD_DIM // 2, 1),
                                    pltpu.roll(blk, HEAD_DIM // 2, 1))
                o = blk * cos + partner * sin
                if c < 3:
                    o = o * (HEAD_DIM ** -0.5)
                parts.append(o)
            res = jnp.concatenate(parts, axis=1)
        if c < 9:
            d = DILATIONS[g]
            qkv_refs[c][...] = res.reshape(d, tm // d, GROUP_WIDTH).astype(BF16)
        elif c < 11:
            _lru_gates(res, c - 9, lru, ext, a_s, b_s)
            _lru_scan(c - 9, a_s, b_s, carry)
        elif c < 13:
            hc = c - 11
            lru_ref[:, hc * half:(hc + 1) * half] = (_gelu_tanh(res) * b_s[hc]).astype(BF16)
        else:
            gl_ref[:, (c - 13) * GROUP_WIDTH:(c - 12) * GROUP_WIDTH] = res.astype(BF16)


def _in_proj(x2, scale, w_bf, cos_t, sin_t, lru_params, batch, seq):
    t = x2.shape[0]
    tm = IN_PROJ_TILE
    n_seq_tiles = seq // tm
    half = LRU_WIDTH // 2
    full = lambda a: pl.BlockSpec(a.shape, lambda i: (0,) * a.ndim)
    qkv_specs, qkv_shapes = [], []
    for c in range(9):
        d = DILATIONS[c % 3]
        qkv_specs.append(pl.BlockSpec((None, d, tm // d, GROUP_WIDTH),
                                      lambda i: (i // n_seq_tiles, 0, i % n_seq_tiles, 0)))
        qkv_shapes.append(jax.ShapeDtypeStruct((batch, d, seq // d, GROUP_WIDTH), BF16))
    return pl.pallas_call(
        functools.partial(_in_proj_kernel, n_seq_tiles=n_seq_tiles),
        grid=(t // tm,),
        in_specs=[
            pl.BlockSpec((tm, D_MODEL), lambda i: (i, 0)),
            pl.BlockSpec((1, D_MODEL), lambda i: (0, 0)),
            pl.BlockSpec((D_MODEL, N_IN), lambda i: (0, 0), pipeline_mode=pl.Buffered(1)),
            pl.BlockSpec((tm, LANES), lambda i: (i % n_seq_tiles, 0)),
            pl.BlockSpec((tm, LANES), lambda i: (i % n_seq_tiles, 0)),
        ] + [full(a) for a in lru_params],
        out_specs=qkv_specs + [
            pl.BlockSpec((tm, LRU_WIDTH), lambda i: (i, 0)),
            pl.BlockSpec((tm, 2 * D_MODEL), lambda i: (i, 0)),
        ],
        out_shape=qkv_shapes + [
            jax.ShapeDtypeStruct((t, LRU_WIDTH), BF16),
            jax.ShapeDtypeStruct((t, 2 * D_MODEL), BF16),
        ],
        scratch_shapes=[
            pltpu.VMEM((D_MODEL // LANES, tm, LANES), F32),
            pltpu.VMEM((3, tm, D_MODEL), BF16),
            pltpu.VMEM((2, tm + 8, half), F32),
            pltpu.VMEM((2, tm, half), F32),
            pltpu.VMEM((2, tm, half), F32),
            pltpu.VMEM((2, SUBLANES, half), F32),
        ],
        compiler_params=_params(("arbitrary",)),
        name="in_proj",
    )(x2, scale, w_bf, cos_t, sin_t, *lru_params)


def _attn_kernel(q_ref, kp_ref, kc_ref, vp_ref, vc_ref, o_ref, lse_ref, kcat, vcat, bias_s, *, tq):
    w = WINDOW_STEPS
    qb = ATTN_QUERY_BLOCK
    kcat[0:w, :] = kp_ref[...]
    kcat[w:w + tq, :] = kc_ref[...]
    kcat[w + tq:, :] = jnp.zeros((qb, GROUP_WIDTH), BF16)
    vcat[0:w, :] = vp_ref[...]
    vcat[w:w + tq, :] = vc_ref[...]
    vcat[w + tq:, :] = jnp.zeros((qb, GROUP_WIDTH), BF16)
    first_tile = pl.program_id(2) == 0

    qi = lax.broadcasted_iota(jnp.int32, (2 * qb, 2 * w), 0) % qb
    col = lax.broadcasted_iota(jnp.int32, (2 * qb, 2 * w), 1)
    band = (col >= qi) & (col <= qi + w)
    bias_s[0] = jnp.where(band, 0.0, -jnp.inf)
    bias_s[1] = jnp.where(band & (col >= w), 0.0, -jnp.inf)
    lane_q = lax.broadcasted_iota(jnp.int32, (qb, LANES), 1)
    low_q = lane_q < HEAD_DIM

    n_pairs = GROUP_WIDTH // LANES

    def query_block(s, carry):
        r0 = s * qb
        bias = bias_s[first_tile.astype(jnp.int32)] if r0 < w else bias_s[0]
        lse_tile = jnp.zeros((qb, LANES), F32)
        for p in range(n_pairs):
            lanes = slice(p * LANES, (p + 1) * LANES)
            qp = q_ref[pl.ds(r0, qb), lanes]
            zero = jnp.zeros_like(qp)
            qs = jnp.concatenate([jnp.where(low_q, qp, zero), jnp.where(low_q, zero, qp)], axis=0)
            kk = kcat[pl.ds(r0, 2 * w), lanes]
            vv = vcat[pl.ds(r0, 2 * w), lanes]
            sc = lax.dot_general(qs, kk, (((1,), (1,)), ((), ())), preferred_element_type=F32)
            sc = sc + bias
            m = jnp.max(sc, axis=1, keepdims=True)
            pr = jnp.exp(sc - m)
            den = jnp.sum(pr, axis=1, keepdims=True)
            pv = jnp.dot(pr.astype(BF16), vv, preferred_element_type=F32) / den
            o_ref[pl.ds(r0, qb), lanes] = jnp.where(low_q, pv[:qb], pv[qb:]).astype(BF16)
            lse = m + jnp.log(den)
            lse_tile = jnp.where(lane_q == 2 * p, lse[:qb], lse_tile)
            lse_tile = jnp.where(lane_q == 2 * p + 1, lse[qb:], lse_tile)
        lse_ref[pl.ds(r0, qb), :] = lse_tile
        return carry

    for s in range(tq // qb):
        query_block(s, 0)


def _attention_group(q, k, v, group):
    batch, d, length, _ = q.shape
    tq = min(ATTN_TILE, length)
    w = WINDOW_STEPS
    sub = tq // w
    cur = pl.BlockSpec((None, None, tq, GROUP_WIDTH), lambda b, r, i: (b, r, i, 0))
    prev = pl.BlockSpec((None, None, w, GROUP_WIDTH),
                        lambda b, r, i: (b, r, jnp.maximum(i * sub - 1, 0), 0))
    return pl.pallas_call(
        functools.partial(_attn_kernel, tq=tq),
        grid=(batch, d, length // tq),
        in_specs=[cur, prev, cur, prev, cur],
        out_specs=[
            pl.BlockSpec((None, None, tq, GROUP_WIDTH), lambda b, r, i: (b, r, i, 0)),
            pl.BlockSpec((None, None, tq, LANES), lambda b, r, i: (b, r, i, 0)),
        ],
        out_shape=[
            jax.ShapeDtypeStruct((batch, d, length, GROUP_WIDTH), BF16),
            jax.ShapeDtypeStruct((batch, d, length, LANES), F32),
        ],
        scratch_shapes=[pltpu.VMEM((tq + w + ATTN_QUERY_BLOCK, GROUP_WIDTH), BF16)] * 2
                       + [pltpu.VMEM((2, 2 * ATTN_QUERY_BLOCK, 2 * w), F32)],
        compiler_params=_params(("arbitrary", "arbitrary", "arbitrary")),
        name=f"attn_d{d}",
    )(q, k, k, v, v)


def _gelu_tanh(x):
    return 0.5 * x * (1.0 + jnp.tanh(0.7978845608028654 * (x + 0.044715 * (x * x * x))))


def _split_bf16(x):
    hi = x.astype(BF16)
    lo = (x - hi.astype(F32)).astype(BF16)
    return hi, lo


def _mix_kernel(x_ref, o1_ref, o2_ref, o3_ref, l1_ref, l2_ref, l3_ref, lru_ref, gl_ref, bm_ref,
                wpa_ref, wpl_ref, wo_ref, n2_ref, expand_ref, rw_ref, rb_ref, tri_ref,
                xo_ref, h2c_ref, meta_ref, cnt_ref, o2_s, o3_s, l2_s, l3_s, cnt_s):
    tm = x_ref.shape[0]

    @pl.when(pl.program_id(0) == 0)
    def _():
        cnt_s[...] = jnp.zeros(cnt_s.shape, F32)

    for src_o, src_l, dst_o, dst_l in ((o2_ref, l2_ref, o2_s, l2_s), (o3_ref, l3_ref, o3_s, l3_s)):
        d = src_o.shape[0]
        for r in range(d):
            o_r = src_o[r].astype(F32)
            for j in range(GROUP_WIDTH // LANES):
                dst_o[j, pl.ds(r, tm // d, stride=d), :] = o_r[:, j * LANES:(j + 1) * LANES]
            dst_l[pl.ds(r, tm // d, stride=d), :] = src_l[r]
    tok = lambda s: jnp.concatenate([s[j] for j in range(GROUP_WIDTH // LANES)], axis=1)
    l1, l2, l3 = l1_ref[0], l2_s[...], l3_s[...]
    mx = jnp.maximum(jnp.maximum(l1, l2), l3)
    e1, e2, e3 = jnp.exp(l1 - mx), jnp.exp(l2 - mx), jnp.exp(l3 - mx)
    inv = 1.0 / (e1 + e2 + e3)
    attn = None
    for e, o_tok in ((e1, o1_ref[0].astype(F32)), (e2, tok(o2_s)), (e3, tok(o3_s))):
        hi, lo = _split_bf16(e * inv)
        wide = jnp.dot(jnp.concatenate([hi, lo], axis=1), expand_ref[...], preferred_element_type=F32)
        term = wide * o_tok
        attn = term if attn is None else attn + term
    pa = jnp.dot(attn.astype(BF16), wpa_ref[...], preferred_element_type=F32)
    pr = jnp.dot(lru_ref[...], wpl_ref[...], preferred_element_type=F32)
    gates = _sigmoid(gl_ref[...].astype(F32) + bm_ref[...])
    mixed = gates[:, :D_MODEL] * pa + gates[:, D_MODEL:] * pr
    x = x_ref[...] + jnp.dot(mixed.astype(BF16), wo_ref[...], preferred_element_type=F32)
    xo_ref[...] = x

    ms = jnp.mean(x * x, axis=-1, keepdims=True)
    h2 = x * lax.rsqrt(ms + EPS) * n2_ref[...]
    h_hi, h_lo = _split_bf16(h2)
    both = jnp.dot(h_hi, rw_ref[...], preferred_element_type=F32)
    logits = (both[:, :LANES] + both[:, LANES:]
              + jnp.dot(h_lo, rw_ref[:, :LANES], preferred_element_type=F32)) + rb_ref[...]
    comb, g_idx = _route(logits)
    h2c_ref[:, :D_MODEL] = h2
    h2c_ref[:, D_MODEL:] = comb

    lane = lax.broadcasted_iota(jnp.int32, (tm, LANES), 1)
    onehot = lane == g_idx
    before = jnp.dot(tri_ref[...], onehot.astype(BF16), preferred_element_type=F32) + cnt_s[0:1, :]
    rank = jnp.sum(jnp.where(onehot, before, 0.0), axis=1, keepdims=True)
    meta_ref[...] = jnp.where(lane == 0, g_idx.astype(F32), jnp.where(lane == 1, rank, 0.0))
    cnt_s[...] = cnt_s[...] + jnp.sum(onehot.astype(F32), axis=0, keepdims=True)
    cnt_ref[...] = cnt_s[...]


def _route(logits):
    tm = logits.shape[0]
    lane = lax.broadcasted_iota(jnp.int32, (tm, LANES), 1)
    neg = -jnp.inf
    big = jnp.int32(LANES)
    is_g = (lane >= N_EXPERTS) & (lane < N_EXPERTS + N_GROUPS)
    gl = jnp.where(is_g, logits, neg)
    gmax = jnp.max(gl, axis=1, keepdims=True)
    g_idx = jnp.min(jnp.where(gl == gmax, lane, big), axis=1, keepdims=True) - N_EXPERTS
    g_w = 1.0 / jnp.sum(jnp.exp(gl - gmax), axis=1, keepdims=True)
    lo = g_idx * EXPERTS_PER_GROUP
    in_grp = (lane >= lo) & (lane < lo + EXPERTS_PER_GROUP)
    el = jnp.where(in_grp, logits, neg)
    v1 = jnp.max(el, axis=1, keepdims=True)
    i1 = jnp.min(jnp.where(el == v1, lane, big), axis=1, keepdims=True)
    el2 = jnp.where(lane == i1, neg, el)
    v2 = jnp.max(el2, axis=1, keepdims=True)
    i2 = jnp.min(jnp.where(el2 == v2, lane, big), axis=1, keepdims=True)
    e2 = jnp.exp(v2 - v1)
    w1 = g_w / (1.0 + e2)
    w2 = g_w * e2 / (1.0 + e2)
    return jnp.where(lane == i1, w1, 0.0) + jnp.where(lane == i2, w2, 0.0), g_idx


def _mix(x2, o_list, lse_list, lru_out, gl, b_merge, wpa, wpl, wo, norm2, expand, rw_split, rb, seq):
    t = x2.shape[0]
    tm = IN_PROJ_TILE
    n_seq_tiles = seq // tm
    row = lambda width: pl.BlockSpec((tm, width), lambda i: (i, 0))
    full = lambda a: pl.BlockSpec(a.shape, lambda i: (0,) * a.ndim)
    resid = lambda d, width: pl.BlockSpec((None, d, tm // d, width),
                                          lambda i: (i // n_seq_tiles, 0, i % n_seq_tiles, 0))
    tri = (jnp.arange(tm)[:, None] > jnp.arange(tm)[None, :]).astype(BF16)
    consts = [b_merge, wpa, wpl, wo, norm2, expand, rw_split, rb, tri]
    return pl.pallas_call(
        _mix_kernel,
        grid=(t // tm,),
        in_specs=[row(D_MODEL)] + [resid(d, GROUP_WIDTH) for d in DILATIONS]
                 + [resid(d, LANES) for d in DILATIONS]
                 + [row(LRU_WIDTH), row(2 * D_MODEL)] + [full(a) for a in consts],
        out_specs=[row(D_MODEL), row(MOE_ROW), row(LANES), pl.BlockSpec((SUBLANES, LANES), lambda i: (0, 0))],
        out_shape=[
            jax.ShapeDtypeStruct((t, D_MODEL), F32),
            jax.ShapeDtypeStruct((t, MOE_ROW), F32),
            jax.ShapeDtypeStruct((t, LANES), F32),
            jax.ShapeDtypeStruct((SUBLANES, LANES), F32),
        ],
        scratch_shapes=[pltpu.VMEM((GROUP_WIDTH // LANES, tm, LANES), F32),
                        pltpu.VMEM((GROUP_WIDTH // LANES, tm, LANES), F32),
                        pltpu.VMEM((tm, LANES), F32), pltpu.VMEM((tm, LANES), F32),
                        pltpu.VMEM((SUBLANES, LANES), F32)],
        compiler_params=_params(("arbitrary",)),
        name="mix",
    )(x2, *o_list, *lse_list, lru_out, gl, *consts)


def _start_row_copies(n_rows, make_copy):
    for r in range(n_rows):
        make_copy(r).start(priority=r % 2)


def _dispatch_kernel(fill_ref, pos_ref, h_ref, xs_ref, zero_row, sem, fill_sem):
    tm = h_ref.shape[0]
    _start_row_copies(tm, lambda r: pltpu.make_async_copy(
        h_ref.at[pl.ds(r, 1), :], xs_ref.at[pl.ds(pos_ref[0, r], 1), :], sem))
    pltpu.make_async_copy(h_ref, xs_ref.at[pl.ds(0, tm), :], sem).wait()

    @pl.when(pl.program_id(0) == pl.num_programs(0) - 1)
    def _():
        zero_row[...] = jnp.zeros(zero_row.shape, F32)
        for g in range(N_GROUPS + 1):
            first, count = fill_ref[2 * g], fill_ref[2 * g + 1]
            pad_copy = lambda r: pltpu.make_async_copy(
                zero_row.at[pl.ds(0, 1), :], xs_ref.at[pl.ds(first + r, 1), :], fill_sem)

            def start(r, c):
                pad_copy(r).start()
                return c

            def wait(r, c):
                pad_copy(r).wait()
                return c

            lax.fori_loop(0, count, start, 0)
            lax.fori_loop(0, count, wait, 0)


def _dispatch(h2c, pos, fill, n_rows):
    t = h2c.shape[0]
    tm = MOE_TOKEN_TILE
    return pl.pallas_call(
        _dispatch_kernel,
        grid_spec=pltpu.PrefetchScalarGridSpec(
            num_scalar_prefetch=1,
            grid=(t // tm,),
            in_specs=[
                pl.BlockSpec((1, tm), lambda i, fill: (0, i), memory_space=pltpu.SMEM),
                pl.BlockSpec((tm, MOE_ROW), lambda i, fill: (i, 0)),
            ],
            out_specs=pl.BlockSpec(memory_space=pl.ANY),
            scratch_shapes=[pltpu.VMEM((SUBLANES, MOE_ROW), F32), pltpu.SemaphoreType.DMA(()),
                            pltpu.SemaphoreType.DMA(())],
        ),
        out_shape=jax.ShapeDtypeStruct((n_rows, MOE_ROW), F32),
        compiler_params=_params(("arbitrary",)),
        name="dispatch",
    )(fill, pos, h2c)


def _expert_kernel(tile_group_ref, n_used_ref, xs_ref, wg_ref, wu_ref, wd_ref, y_ref):
    j = pl.program_id(0)

    @pl.when(j < n_used_ref[0])
    def _():
        h = xs_ref[:, :D_MODEL].astype(BF16)
        comb = xs_ref[:, D_MODEL:]
        lane = lax.broadcasted_iota(jnp.int32, comb.shape, 1)
        first = tile_group_ref[j] * EXPERTS_PER_GROUP
        y = None
        for e in range(EXPERTS_PER_GROUP):
            gate = jnp.dot(h, wg_ref[e], preferred_element_type=F32)
            up = jnp.dot(h, wu_ref[e], preferred_element_type=F32)
            w_e = jnp.sum(jnp.where(lane == first + e, comb, 0.0), axis=1, keepdims=True)
            act = (gate * _sigmoid(gate) * up * w_e).astype(BF16)
            part = jnp.dot(act, wd_ref[e], preferred_element_type=F32)
            y = part if y is None else y + part
        y_ref[...] = y

    @pl.when(j >= n_used_ref[0])
    def _():
        y_ref[...] = jnp.zeros(y_ref.shape, F32)


def _experts(xs, tile_group, n_used, wg, wu, wd):
    n_rows = xs.shape[0]
    tg = MOE_GROUP_TILE
    epg = EXPERTS_PER_GROUP
    return pl.pallas_call(
        _expert_kernel,
        grid_spec=pltpu.PrefetchScalarGridSpec(
            num_scalar_prefetch=2,
            grid=(n_rows // tg,),
            in_specs=[
                pl.BlockSpec((tg, MOE_ROW), lambda j, tgr, nu: (jnp.minimum(j, nu[0] - 1), 0)),
                pl.BlockSpec((epg, D_MODEL, D_EXPERT), lambda j, tgr, nu: (tgr[j], 0, 0)),
                pl.BlockSpec((epg, D_MODEL, D_EXPERT), lambda j, tgr, nu: (tgr[j], 0, 0)),
                pl.BlockSpec((epg, D_EXPERT, D_MODEL), lambda j, tgr, nu: (tgr[j], 0, 0)),
            ],
            out_specs=pl.BlockSpec((tg, D_MODEL), lambda j, tgr, nu: (j, 0)),
        ),
        out_shape=jax.ShapeDtypeStruct((n_rows, D_MODEL), F32),
        compiler_params=_params(("arbitrary",)),
        name="experts",
    )(tile_group, n_used, xs, wg, wu, wd)


def _combine_kernel(pos_ref, x_ref, y_ref, fn_ref, out_ref, ybuf, sem, *, final_norm):
    tm = ybuf.shape[0]
    _start_row_copies(tm, lambda r: pltpu.make_async_copy(
        y_ref.at[pl.ds(pos_ref[0, r], 1), :], ybuf.at[pl.ds(r, 1), :], sem))
    pltpu.make_async_copy(y_ref.at[pl.ds(0, tm), :], ybuf, sem).wait()
    x = x_ref[...] + ybuf[...]
    if final_norm:
        ms = jnp.mean(x * x, axis=-1, keepdims=True)
        x = x * lax.rsqrt(ms + EPS) * fn_ref[...]
    out_ref[...] = x


def _combine(x_mid, y, pos, final_scale, final_norm):
    t = x_mid.shape[0]
    tm = MOE_TOKEN_TILE
    return pl.pallas_call(
        functools.partial(_combine_kernel, final_norm=final_norm),
        grid=(t // tm,),
        in_specs=[
            pl.BlockSpec((1, tm), lambda i: (0, i), memory_space=pltpu.SMEM),
            pl.BlockSpec((tm, D_MODEL), lambda i: (i, 0)),
            pl.BlockSpec(memory_space=pl.ANY),
            pl.BlockSpec((1, D_MODEL), lambda i: (0, 0)),
        ],
        out_specs=pl.BlockSpec((tm, D_MODEL), lambda i: (i, 0)),
        out_shape=jax.ShapeDtypeStruct((t, D_MODEL), F32),
        scratch_shapes=[pltpu.VMEM((tm, D_MODEL), F32), pltpu.SemaphoreType.DMA(())],
        compiler_params=_params(("arbitrary",)),
        name="combine",
    )(pos, x_mid, y, final_scale)


def _moe(x_mid, h2c, meta, counts, wg, wu, wd, final_scale, final_norm):
    t = x_mid.shape[0]
    tg = MOE_GROUP_TILE
    n_rows = t + N_GROUPS * tg
    cnt = counts[0, :N_GROUPS].astype(jnp.int32)
    padded = (cnt + tg - 1) // tg * tg
    ends = jnp.cumsum(padded)
    starts = ends - padded
    pos = (starts[meta[:, 0].astype(jnp.int32)] + meta[:, 1].astype(jnp.int32)).reshape(1, t)
    tile_start = jnp.arange(n_rows // tg, dtype=jnp.int32) * tg
    tile_group = jnp.minimum(jnp.sum(tile_start[:, None] >= ends[None, :], axis=1), N_GROUPS - 1)
    n_used = (ends[-1] // tg).reshape(1)
    fill = jnp.stack([jnp.append(starts + cnt, ends[-1]), jnp.append(padded - cnt, n_rows - ends[-1])],
                     axis=1).reshape(-1)
    xs = _dispatch(h2c, pos, fill.astype(jnp.int32), n_rows)
    y = _experts(xs, tile_group.astype(jnp.int32), n_used.astype(jnp.int32), wg, wu, wd)
    return _combine(x_mid, y, pos, final_scale, final_norm)


def _rotary_tables(seq):
    inv = ROPE_THETA ** (-jnp.arange(0, HEAD_DIM, 2, dtype=F32) / HEAD_DIM)
    ang = jnp.arange(seq, dtype=F32)[:, None] * inv[None, :]
    cos, sin = jnp.cos(ang), jnp.sin(ang)
    return jnp.tile(cos, (1, 4)), jnp.concatenate([-sin, sin, -sin, sin], axis=1)


def _block_diag(w):
    nb = w.shape[0] // 4
    w4 = w.reshape(nb, 4, LRU_BLOCK, LRU_BLOCK)
    eye = jnp.eye(4, dtype=w.dtype)
    return jnp.einsum('qaij,ab->qaibj', w4, eye).reshape(nb, 4 * LRU_BLOCK, 4 * LRU_BLOCK)


def kernel(x, norm1_scale, w_in, b_merge, conv_w, conv_b, lru_wa, lru_ba, lru_wx, lru_bx, lru_lambda,
           w_proj_attn, w_proj_lru, w_out, norm2_scale, router_group_w, router_group_b, router_expert_w,
           router_expert_b, expert_w_gate, expert_w_up, expert_w_down, final_norm_scale):
    batch, seq, _ = x.shape
    depth = w_in.shape[0]
    cos_t, sin_t = _rotary_tables(seq)
    expand = (jnp.arange(LANES)[:, None] == (jnp.arange(GROUP_WIDTH)[None, :] // HEAD_DIM)).astype(BF16)
    expand = jnp.concatenate([expand, expand], axis=0)
    row = lambda v: v.reshape(1, -1)
    x2 = x.reshape(batch * seq, D_MODEL)
    for l in range(depth):
        lru_params = (conv_w[l], row(conv_b[l]), _block_diag(lru_wa[l]).astype(BF16),
                      _block_diag(lru_wx[l]).astype(BF16), row(lru_ba[l]), row(lru_bx[l]), row(lru_lambda[l]))
        *qkv, lru_out, gl = _in_proj(x2, row(norm1_scale[l]), w_in[l].astype(BF16), cos_t, sin_t, lru_params,
                                     batch, seq)
        o_list, lse_list = [], []
        for g in range(len(DILATIONS)):
            o, lse = _attention_group(qkv[g], qkv[3 + g], qkv[6 + g], g)
            o_list.append(o)
            lse_list.append(lse)
        rw = jnp.zeros((D_MODEL, LANES), F32)
        rw = rw.at[:, :N_EXPERTS].set(router_expert_w[l]).at[:, N_EXPERTS:N_EXPERTS + N_GROUPS].set(
            router_group_w[l])
        rb = jnp.zeros((1, LANES), F32)
        rb = rb.at[0, :N_EXPERTS].set(router_expert_b[l]).at[0, N_EXPERTS:N_EXPERTS + N_GROUPS].set(
            router_group_b[l])
        rw_hi = rw.astype(BF16)
        rw_split = jnp.concatenate([rw_hi, (rw - rw_hi.astype(F32)).astype(BF16)], axis=1)
        x_mid, h2c, meta, counts = _mix(x2, o_list, lse_list, lru_out, gl, row(b_merge[l]),
                                        w_proj_attn[l].astype(BF16), w_proj_lru[l].astype(BF16),
                                        w_out[l].astype(BF16), row(norm2_scale[l]), expand, rw_split, rb, seq)
        x2 = _moe(x_mid, h2c, meta, counts, expert_w_gate[l].astype(BF16), expert_w_up[l].astype(BF16),
                  expert_w_down[l].astype(BF16), row(final_norm_scale), final_norm=(l == depth - 1))
    return x2.reshape(batch, seq, D_MODEL)
```
